```python
import jax
import jax.numpy as jnp
from jax import lax
import numpy as np

D_MODEL = 2048
BATCH = 4
SEQ = 4096
DEPTH = 4

N_META = 16
BLOCK = 128
FRONT_PAD = (-N_META) % BLOCK
N_A_LAYERS = DEPTH // 2
N_B_LAYERS = DEPTH - N_A_LAYERS
RWKV_HEAD = 64
RWKV_HEADS = D_MODEL // RWKV_HEAD
LORA_DECAY = 96
LORA_ICLR = 96
LORA_VRES = 64
LORA_GATE = 256
SB_HEAD = 128
SB_HEADS = D_MODEL // SB_HEAD
D_FF = -(-8 * D_MODEL // (3 * 256)) * 256
RMS_EPS = 1e-6
GN_EPS = 64e-5

kernel_name = 'hybrid_rwkv7_stickbreaking_yoco'


def _rms_norm(x, g):
    xf = x.astype(jnp.float32)
    y = xf * lax.rsqrt(jnp.mean(xf * xf, axis=-1, keepdims=True) + RMS_EPS)
    return (y * g.astype(jnp.float32)).astype(x.dtype)


def _swiglu(x, w_gate, w_up, w_down):
    return (jax.nn.silu(x @ w_gate) * (x @ w_up)) @ w_down


def _token_shift(x):
    return jnp.pad(x, ((0, 0), (1, 0), (0, 0)))[:, :-1]


def _rwkv7_time_mix(h, v_first, mu, w_r, w_k, w_v, w_o, dec_w0, dec_w1, dec_w2,
                    a_w0, a_w1, a_w2, g_w1, g_w2, k_k, k_a, r_k, gn_w, gn_b, vres):
    b, l, d = h.shape
    xx = _token_shift(h) - h
    xr, xw, xk, xv, xa, xg = (h + xx * mu[i] for i in range(6))
    r = xr @ w_r
    k = xk @ w_k
    v = xv @ w_v
    w = -jax.nn.softplus(-(dec_w0 + jnp.tanh(xw @ dec_w1) @ dec_w2)) - 0.5
    decay = jnp.exp(-jnp.exp(w.astype(jnp.float32)))
    a = jax.nn.sigmoid(a_w0 + (xa @ a_w1) @ a_w2)
    g = jax.nn.sigmoid(xg @ g_w1) @ g_w2
    if vres is None:
        v_first = v
    else:
        v_w0, v_w1, v_w2 = vres
        v = v + (v_first - v) * jax.nn.sigmoid(v_w0 + (xv @ v_w1) @ v_w2)
    heads = lambda t: t.reshape(b, l, RWKV_HEADS, RWKV_HEAD).astype(jnp.float32)
    kk = heads(k * k_k)
    kk = kk * lax.rsqrt(jnp.maximum(jnp.sum(kk * kk, axis=-1, keepdims=True), 1e-24))
    k = k * (1.0 + (a - 1.0) * k_a)
    rh, kh, vh, ah, dh = heads(r), heads(k), heads(v), heads(a), heads(decay)

    def step(state, inp):
        r_t, d_t, k_t, v_t, kk_t, a_t = inp
        sa = jnp.einsum('bhvk,bhk->bhv', state, -kk_t)
        state = (state * d_t[:, :, None, :]
                 + sa[..., None] * (kk_t * a_t)[:, :, None, :]
                 + v_t[..., None] * k_t[:, :, None, :])
        return state, jnp.einsum('bhvk,bhk->bhv', state, r_t)

    seq_first = lambda t: jnp.swapaxes(t, 0, 1)
    s0 = jnp.zeros((b, RWKV_HEADS, RWKV_HEAD, RWKV_HEAD), jnp.float32)
    _, y = lax.scan(step, s0, tuple(seq_first(t) for t in (rh, dh, kh, vh, kk, ah)))
    y = seq_first(y)
    mean = jnp.mean(y, axis=-1, keepdims=True)
    var = jnp.mean(jnp.square(y - mean), axis=-1, keepdims=True)
    y = ((y - mean) * lax.rsqrt(var + GN_EPS)).reshape(b, l, d) * gn_w + gn_b
    bonus = (jnp.sum(rh * kh * r_k, axis=-1, keepdims=True) * vh).reshape(b, l, d)
    out = ((y + bonus).astype(h.dtype) * g) @ w_o
    return out, v_first


def _pad_front_heads(t):
    return jnp.pad(t, ((0, 0), (FRONT_PAD, 0), (0, 0), (0, 0))).transpose(0, 2, 1, 3)


def _shared_kv(h, w_k, w_v, k_gain):
    b, l, _ = h.shape
    k = _rms_norm((h @ w_k).reshape(b, l, SB_HEADS, SB_HEAD), k_gain)
    v = (h @ w_v).reshape(b, l, SB_HEADS, SB_HEAD)
    return _pad_front_heads(k), _pad_front_heads(v)


def _stick_breaking(h, k, v, w_q, q_gain, w_o):
    b, l, d = h.shape
    q = _pad_front_heads(_rms_norm((h @ w_q).reshape(b, l, SB_HEADS, SB_HEAD), q_gain))
    p = l + FRONT_PAD
    nb = p // BLOCK
    q_blocks = q.reshape(b, SB_HEADS, nb, BLOCK, SB_HEAD).transpose(2, 0, 1, 3, 4)
    key_pos = jnp.arange(p)
    scale = SB_HEAD ** -0.5

    def one_block(args):
        q_blk, blk = args
        q_pos = blk * BLOCK + jnp.arange(BLOCK)
        valid = (key_pos[None, :] < q_pos[:, None]) & (key_pos[None, :] >= FRONT_PAD)
        z = jnp.einsum('bhqd,bhkd->bhqk', q_blk, k).astype(jnp.float32) * scale
        log_keep = jnp.where(valid, -jax.nn.softplus(z), 0.0)
        later = lax.cumsum(log_keep, axis=3, reverse=True) - log_keep
        weights = jnp.where(valid, jnp.exp(jax.nn.log_sigmoid(z) + later), 0.0)
        return jnp.einsum('bhqk,bhkd->bhqd', weights.astype(v.dtype), v)

    o = lax.map(one_block, (q_blocks, jnp.arange(nb)))
    o = o.transpose(1, 0, 3, 2, 4).reshape(b, p, d)[:, FRONT_PAD:]
    return o @ w_o


def setup_inputs(seed: int = 0) -> dict:
    key = jax.random.key(seed)
    ks = iter(jax.random.split(key, 48))
    f32 = jnp.float32

    def nrm(shape, scale):
        return jax.random.normal(next(ks), shape, f32) * scale

    def unif(shape, lo, hi):
        return jax.random.uniform(next(ks), shape, f32, lo, hi)

    D, F, NA, NB = D_MODEL, D_FF, N_A_LAYERS, N_B_LAYERS
    d_in = D ** -0.5
    d_out = D ** -0.5 * (2 * DEPTH) ** -0.5
    return {
        'x': nrm((BATCH, SEQ, D), 1.0),
        'meta_tokens': nrm((N_META, D), 1.0),
        'mix_norm_g': 1.0 + nrm((DEPTH, D), 0.02),
        'ffn_norm_g': 1.0 + nrm((DEPTH, D), 0.02),
        'ffn_w_gate': nrm((DEPTH, D, F), d_in),
        'ffn_w_up': nrm((DEPTH, D, F), d_in),
        'ffn_w_down': nrm((DEPTH, F, D), F ** -0.5 * (2 * DEPTH) ** -0.5),
        'rwkv_mu': unif((NA, 6, D), 0.0, 1.0),
        'rwkv_w_r': nrm((NA, D, D), d_in),
        'rwkv_w_k': nrm((NA, D, D), d_in),
        'rwkv_w_v': nrm((NA, D, D), d_in),
        'rwkv_w_o': nrm((NA, D, D), d_out),
        'rwkv_dec_w0': unif((NA, D), -6.0, -1.0),
        'rwkv_dec_w1': nrm((NA, D, LORA_DECAY), d_in),
        'rwkv_dec_w2': nrm((NA, LORA_DECAY, D), 0.1 * LORA_DECAY ** -0.5),
        'rwkv_a_w0': nrm((NA, D), 0.1),
        'rwkv_a_w1': nrm((NA, D, LORA_ICLR), d_in),
        'rwkv_a_w2': nrm((NA, LORA_ICLR, D), 0.1 * LORA_ICLR ** -0.5),
        'rwkv_g_w1': nrm((NA, D, LORA_GATE), d_in),
        'rwkv_g_w2': nrm((NA, LORA_GATE, D), LORA_GATE ** -0.5),
        'rwkv_k_k': 0.85 + nrm((NA, D), 0.02),
        'rwkv_k_a': 1.0 + nrm((NA, D), 0.02),
        'rwkv_r_k': nrm((NA, RWKV_HEADS, RWKV_HEAD), 0.1),
        'rwkv_gn_w': 1.0 + nrm((NA, D), 0.02),
        'rwkv_gn_b': nrm((NA, D), 0.02),
        'rwkv_v_w0': 1.0 + nrm((NA - 1, D), 0.1),
        'rwkv_v_w1': nrm((NA - 1, D, LORA_VRES), d_in),
        'rwkv_v_w2': nrm((NA - 1, LORA_VRES, D), 0.1 * LORA_VRES ** -0.5),
        'kv_norm_g': 1.0 + nrm((D,), 0.02),
        'sb_w_k': nrm((D, D), d_in),
        'sb_w_v': nrm((D, D), d_in),
        'sb_k_gain': 1.0 + nrm((SB_HEAD,), 0.02),
        'sb_w_q': nrm((NB, D, D), d_in),
        'sb_q_gain': 1.0 + nrm((NB, SB_HEAD), 0.02),
        'sb_w_o': nrm((NB, D, D), d_out),
    }


def reference(x, meta_tokens, mix_norm_g, ffn_norm_g, ffn_w_gate, ffn_w_up, ffn_w_down,
              rwkv_mu, rwkv_w_r, rwkv_w_k, rwkv_w_v, rwkv_w_o, rwkv_dec_w0, rwkv_dec_w1,
              rwkv_dec_w2, rwkv_a_w0, rwkv_a_w1, rwkv_a_w2, rwkv_g_w1, rwkv_g_w2, rwkv_k_k,
              rwkv_k_a, rwkv_r_k, rwkv_gn_w, rwkv_gn_b, rwkv_v_w0, rwkv_v_w1, rwkv_v_w2,
              kv_norm_g, sb_w_k, sb_w_v, sb_k_gain, sb_w_q, sb_q_gain, sb_w_o):
    b = x.shape[0]
    meta = jnp.broadcast_to(meta_tokens[None].astype(x.dtype), (b, N_META, D_MODEL))
    h = jnp.concatenate([meta, x], axis=1)
    v_first = None
    k_shared = v_shared = None
    for layer in range(DEPTH):
        hn = _rms_norm(h, mix_norm_g[layer])
        if layer < N_A_LAYERS:
            i = layer
            vres = None if i == 0 else (rwkv_v_w0[i - 1], rwkv_v_w1[i - 1], rwkv_v_w2[i - 1])
            mix, v_first = _rwkv7_time_mix(
                hn, v_first, rwkv_mu[i], rwkv_w_r[i], rwkv_w_k[i], rwkv_w_v[i], rwkv_w_o[i],
                rwkv_dec_w0[i], rwkv_dec_w1[i], rwkv_dec_w2[i], rwkv_a_w0[i], rwkv_a_w1[i],
                rwkv_a_w2[i], rwkv_g_w1[i], rwkv_g_w2[i], rwkv_k_k[i], rwkv_k_a[i], rwkv_r_k[i],
                rwkv_gn_w[i], rwkv_gn_b[i], vres)
        else:
            j = layer - N_A_LAYERS
            if j == 0:
                k_shared, v_shared = _shared_kv(_rms_norm(h, kv_norm_g), sb_w_k, sb_w_v, sb_k_gain)
            mix = _stick_breaking(hn, k_shared, v_shared, sb_w_q[j], sb_q_gain[j], sb_w_o[j])
        h = h + mix
        h = h + _swiglu(_rms_norm(h, ffn_norm_g[layer]), ffn_w_gate[layer], ffn_w_up[layer],
                        ffn_w_down[layer])
    return h[:, N_META:]
```

```python
import functools

import jax
import jax.numpy as jnp
from jax import lax
from jax.experimental import pallas as pl
from jax.experimental.pallas import tpu as pltpu

N_META = 16
BLOCK = 128
RWKV_HEAD = 64
SB_HEAD = 128
RMS_EPS = 1e-6
GN_EPS = 64e-5

LANES = 128
RWKV_CHUNK = 64
VMEM_LIMIT_BYTES = 56 * 1024 * 1024

F32 = jnp.float32
BF16 = jnp.bfloat16


def _largest_tile(n, cap, quantum):
    best = None
    for t in range(quantum, min(n, cap) + 1, quantum):
        if n % t == 0:
            best = t
    assert best is not None, (n, cap, quantum)
    return best


def _params(*sem):
    return pltpu.CompilerParams(dimension_semantics=sem, vmem_limit_bytes=VMEM_LIMIT_BYTES)


def _mm_kernel(x_ref, w_ref, o_ref):
    o_ref[...] = jnp.dot(x_ref[...], w_ref[...], preferred_element_type=F32).astype(o_ref.dtype)


def _matmul(x, w, out_dtype=F32):
    m, k = x.shape
    n = w.shape[1]
    tm = _largest_tile(m, 1056, 16)
    tn = _largest_tile(n, 1024, LANES)
    return pl.pallas_call(
        _mm_kernel,
        out_shape=jax.ShapeDtypeStruct((m, n), out_dtype),
        grid=(n // tn, m // tm),
        in_specs=[pl.BlockSpec((tm, k), lambda j, i: (i, 0)),
                  pl.BlockSpec((k, tn), lambda j, i: (0, j))],
        out_specs=pl.BlockSpec((tm, tn), lambda j, i: (i, j)),
        compiler_params=_params("parallel", "parallel"),
        name="matmul",
    )(x.astype(BF16), w.astype(BF16))


def _ffn_kernel(h_ref, g_ref, wg_ref, wu_ref, wd_ref, o_ref, xn_ref, *, tiles_per_seq, front_pad):
    f = pl.program_id(1)

    @pl.when(f == 0)
    def _():
        h = h_ref[...]
        ms = jnp.mean(h * h, axis=-1, keepdims=True)
        xn_ref[...] = (h * lax.rsqrt(ms + RMS_EPS) * g_ref[...]).astype(BF16)
        o_ref[...] = h

    xn = xn_ref[...]
    gate = jnp.dot(xn, wg_ref[...], preferred_element_type=F32)
    up = jnp.dot(xn, wu_ref[...], preferred_element_type=F32)
    act = (gate * jax.nn.sigmoid(gate) * up).astype(BF16)
    o_ref[...] += jnp.dot(act, wd_ref[...], preferred_element_type=F32)

    @pl.when(f == pl.num_programs(1) - 1)
    def _():
        tm = o_ref.shape[0]
        row = lax.rem(pl.program_id(0), tiles_per_seq) * tm + lax.broadcasted_iota(jnp.int32, (tm, 1), 0)
        o_ref[...] = jnp.where(row >= front_pad, o_ref[...], 0.0)


def _ffn(h, g, wg, wu, wd, *, seq_rows, front_pad):
    m, d = h.shape
    dff = wg.shape[1]
    tm = _largest_tile(seq_rows, 528, 16)
    tf = _largest_tile(dff, 512, LANES)
    kern = functools.partial(_ffn_kernel, tiles_per_seq=seq_rows // tm, front_pad=front_pad)
    return pl.pallas_call(
        kern,
        out_shape=jax.ShapeDtypeStruct((m, d), F32),
        grid=(m // tm, dff // tf),
        in_specs=[pl.BlockSpec((tm, d), lambda i, f: (i, 0)),
                  pl.BlockSpec((1, d), lambda i, f: (0, 0)),
                  pl.BlockSpec((d, tf), lambda i, f: (0, f)),
                  pl.BlockSpec((d, tf), lambda i, f: (0, f)),
                  pl.BlockSpec((tf, d), lambda i, f: (f, 0))],
        out_specs=pl.BlockSpec((tm, d), lambda i, f: (i, 0)),
        scratch_shapes=[pltpu.VMEM((tm, d), BF16)],
        compiler_params=_params("parallel", "arbitrary"),
        name="swiglu_block",
    )(h, g.reshape(1, d), wg.astype(BF16), wu.astype(BF16), wd.astype(BF16))


def _bdot(a, b):
    return jnp.dot(a.astype(BF16), b.astype(BF16), preferred_element_type=F32)


def _bdot_nt(a, b):
    return lax.dot_general(a.astype(BF16), b.astype(BF16), (((1,), (1,)), ((), ())),
                           preferred_element_type=F32)


def _split3(x):
    hi = x.astype(BF16)
    r1 = x - hi.astype(F32)
    mid = r1.astype(BF16)
    lo = (r1 - mid.astype(F32)).astype(BF16)
    return hi, mid, lo


def _rwkv_pair_chunk(r, wl, k, v, kk, al, s_bd, consts):
    lane_h0, l_incl, m_strict, m_incl, eye, level_masks = consts
    c = r.shape[0]

    lw = -jnp.exp(wl)
    hi, mid, lo = _split3(lw)
    cum = (jnp.dot(l_incl, hi, preferred_element_type=F32)
           + jnp.dot(l_incl, mid, preferred_element_type=F32)
           + jnp.dot(l_incl, lo, preferred_element_type=F32))
    total = cum[c - 1:c, :]
    e_pos = jnp.exp(cum)
    e_prev = jnp.exp(cum - lw)
    e_neg = jnp.exp(-cum)
    e_rem = jnp.exp(total - cum)

    b = kk * al

    def stack(x):
        return jnp.concatenate([jnp.where(lane_h0, x, 0.0), jnp.where(lane_h0, 0.0, x)], axis=0)

    a_s = stack(-kk * e_prev)
    r_s = stack(r * e_pos)
    v_s = stack(v)
    bh_s = stack(b * e_rem)
    kh_s = stack(k * e_rem)
    bt = b * e_neg
    kt = k * e_neg

    gram = _bdot_nt(jnp.concatenate([a_s, r_s], axis=0),
                    jnp.concatenate([bt, bt, kt, kt], axis=0))
    c2 = 2 * c
    n = jnp.where(m_strict, gram[:c2, :c2], 0.0)
    a_ak = jnp.where(m_strict, gram[:c2, c2:], 0.0)
    a_rb = jnp.where(m_incl, gram[c2:, :c2], 0.0)
    a_rk = jnp.where(m_incl, gram[c2:, c2:], 0.0)

    same8 = level_masks[0]
    n0 = jnp.where(same8, n, 0.0)
    p2 = _bdot(n0, n0)
    p4 = _bdot(p2, p2)
    x = eye + n0 + p2 + _bdot(n0, p2)
    x = x + _bdot(x, p4)
    for off_mask in level_masks[1:]:
        x = x + _bdot(_bdot(x, jnp.where(off_mask, n, 0.0)), x)

    sa = _bdot(x, _bdot_nt(a_s, s_bd) + _bdot(a_ak, v_s))
    y_s = _bdot_nt(r_s, s_bd) + _bdot(a_rb, sa) + _bdot(a_rk, v_s)
    y = y_s[:c, :] + y_s[c:, :]

    upd = jnp.dot(jnp.concatenate([sa, v_s], axis=0).T.astype(BF16),
                  jnp.concatenate([bh_s, kh_s], axis=0).astype(BF16),
                  preferred_element_type=F32)
    s_new = s_bd * jnp.exp(total) + upd
    return y, s_new


def _rwkv_consts(c):
    c2 = 2 * c
    row = lax.broadcasted_iota(jnp.int32, (c2, c2), 0)
    col = lax.broadcasted_iota(jnp.int32, (c2, c2), 1)

    def same(size):
        sh = size.bit_length() - 1
        return lax.shift_right_logical(row, sh) == lax.shift_right_logical(col, sh)

    same_head = same(c)
    m_strict = same_head & (col < row)
    m_incl = same_head & (col <= row)
    eye = jnp.where(row == col, 1.0, 0.0).astype(F32)
    levels = [same(8)]
    size = 16
    while size <= c:
        levels.append(same(size) & jnp.logical_not(same(size // 2)))
        size *= 2
    lane_h0 = lax.broadcasted_iota(jnp.int32, (c, LANES), 1) < RWKV_HEAD
    rr = lax.broadcasted_iota(jnp.int32, (c, c), 0)
    cc = lax.broadcasted_iota(jnp.int32, (c, c), 1)
    l_incl = jnp.where(cc <= rr, 1.0, 0.0).astype(BF16)
    return lane_h0, l_incl, m_strict, m_incl, eye, levels


def _rwkv_kernel(r_ref, w_ref, k_ref, v_ref, kk_ref, a_ref, y_ref, s_ref, *, pairs):
    @pl.when(pl.program_id(2) == 0)
    def _():
        s_ref[...] = jnp.zeros_like(s_ref)

    consts = _rwkv_consts(r_ref.shape[1])
    for p in range(pairs):
        sl = slice(p * LANES, (p + 1) * LANES)
        y, s_new = _rwkv_pair_chunk(r_ref[0, :, sl], w_ref[0, :, sl], k_ref[0, :, sl], v_ref[0, :, sl],
                                    kk_ref[0, :, sl], a_ref[0, :, sl], s_ref[p], consts)
        y_ref[0, :, sl] = y
        s_ref[p] = s_new


def _rwkv_scan(r, wl, k, v, kk, al):
    bsz, p_rows, d = r.shape
    c = RWKV_CHUNK
    pairs = 4 if d % (4 * LANES) == 0 else 1
    width = pairs * LANES
    spec = pl.BlockSpec((1, c, width), lambda b, g, t: (b, t, g))
    return pl.pallas_call(
        functools.partial(_rwkv_kernel, pairs=pairs),
        out_shape=jax.ShapeDtypeStruct((bsz, p_rows, d), F32),
        grid=(bsz, d // width, p_rows // c),
        in_specs=[spec] * 6,
        out_specs=spec,
        scratch_shapes=[pltpu.VMEM((pairs, LANES, LANES), F32)],
        compiler_params=_params("parallel", "parallel", "arbitrary"),
        name="rwkv7_scan",
    )(r, wl, k, v, kk, al)


def _sb_kernel(q_ref, k_ref, v_ref, o_ref, *, front_pad, scale):
    i = pl.program_id(2)
    blk = q_ref.shape[1]
    q = q_ref[0]
    rr = lax.broadcasted_iota(jnp.int32, (blk, 2 * blk), 0)
    cc = lax.broadcasted_iota(jnp.int32, (blk, 2 * blk), 1)
    suffix_ones = jnp.where((cc >= blk) | (rr > cc), 1.0, 0.0).astype(BF16)
    q_pos = i * blk + lax.broadcasted_iota(jnp.int32, (blk, blk), 0)
    lane = lax.broadcasted_iota(jnp.int32, (blk, blk), 1)

    def body(jj, carry):
        acc, later_blocks = carry
        j = i - jj
        start = pl.multiple_of(j * blk, blk)
        kb = k_ref[0, pl.ds(start, blk), :]
        vb = v_ref[0, pl.ds(start, blk), :]
        z = lax.dot_general(q, kb, (((1,), (1,)), ((), ())), preferred_element_type=F32) * scale
        k_pos = j * blk + lane
        valid = (k_pos < q_pos) & (k_pos >= front_pad)
        sp = jnp.maximum(z, 0.0) + jnp.log1p(jnp.exp(-jnp.abs(z)))
        log_keep = jnp.where(valid, -sp, 0.0)
        hi = log_keep.astype(BF16)
        lo = (log_keep - hi.astype(F32)).astype(BF16)
        cs = (jnp.dot(hi, suffix_ones, preferred_element_type=F32)
              + jnp.dot(lo, suffix_ones, preferred_element_type=F32))
        later = cs[:, :blk] + later_blocks
        w = jnp.where(valid, jnp.exp(z - sp + later), 0.0)
        acc = acc + jnp.dot(w.astype(BF16), vb, preferred_element_type=F32)
        return acc, later_blocks + cs[:, blk:]

    zeros = jnp.zeros((blk, blk), F32)
    acc, _ = lax.fori_loop(0, i + 1, body, (zeros, zeros))
    o_ref[0] = acc.astype(o_ref.dtype)


def _sb_attention(q, k, v, *, front_pad):
    bsz, p_rows, d = q.shape
    blk = BLOCK
    kern = functools.partial(_sb_kernel, front_pad=front_pad, scale=SB_HEAD ** -0.5)
    kv_spec = pl.BlockSpec((1, p_rows, SB_HEAD), lambda b, h, i: (b, 0, h))
    return pl.pallas_call(
        kern,
        out_shape=jax.ShapeDtypeStruct((bsz, p_rows, d), BF16),
        grid=(bsz, d // SB_HEAD, p_rows // blk),
        in_specs=[pl.BlockSpec((1, blk, SB_HEAD), lambda b, h, i: (b, i, h)), kv_spec, kv_spec],
        out_specs=pl.BlockSpec((1, blk, SB_HEAD), lambda b, h, i: (b, i, h)),
        compiler_params=_params("parallel", "parallel", "arbitrary"),
        name="stick_breaking_attention",
    )(q, k, v)


def _rms(x, g):
    return x * lax.rsqrt(jnp.mean(x * x, axis=-1, keepdims=True) + RMS_EPS) * g


def _head_rms(x, g, head):
    shp = x.shape
    xh = x.reshape(shp[:-1] + (shp[-1] // head, head))
    return _rms(xh, g).reshape(shp)


def _pad_cols(w, n):
    return jnp.pad(w, ((0, 0), (0, n - w.shape[1])))


def _pad_rows(w, n):
    return jnp.pad(w, ((0, n - w.shape[0]), (0, 0)))


def _lora(x, w1, w2, act):
    rank = w1.shape[1]
    rp = -(-rank // LANES) * LANES
    mid = _matmul(x, _pad_cols(w1, rp))
    return _matmul(act(mid), _pad_rows(w2, rp))


def _rwkv_time_mix(hn3, v_first, mu, w_r, w_k, w_v, w_o, dec_w0, dec_w1, dec_w2, a_w0, a_w1, a_w2,
                   g_w1, g_w2, k_k, k_a, r_k, gn_w, gn_b, vres):
    bsz, p_rows, d = hn3.shape
    t = bsz * p_rows
    xx = jnp.pad(hn3, ((0, 0), (1, 0), (0, 0)))[:, :-1] - hn3
    xr, xw, xk, xv, xa, xg = ((hn3 + xx * mu[i]).reshape(t, d).astype(BF16) for i in range(6))
    r = _matmul(xr, w_r)
    k = _matmul(xk, w_k)
    v = _matmul(xv, w_v)
    wl = -jax.nn.softplus(-(dec_w0 + _lora(xw, dec_w1, dec_w2, jnp.tanh))) - 0.5
    a = jax.nn.sigmoid(a_w0 + _lora(xa, a_w1, a_w2, lambda u: u))
    g = _lora(xg, g_w1, g_w2, jax.nn.sigmoid)
    if vres is None:
        v_first = v
    else:
        v_w0, v_w1, v_w2 = vres
        v = v + (v_first - v) * jax.nn.sigmoid(v_w0 + _lora(xv, v_w1, v_w2, lambda u: u))
    heads = d // RWKV_HEAD
    kk = (k * k_k).reshape(t, heads, RWKV_HEAD)
    kk = (kk * lax.rsqrt(jnp.maximum(jnp.sum(kk * kk, axis=-1, keepdims=True), 1e-24))).reshape(t, d)
    k = k * (1.0 + (a - 1.0) * k_a)
    to3 = lambda u: u.reshape(bsz, p_rows, d)
    y = _rwkv_scan(to3(r), to3(wl), to3(k), to3(v), to3(kk), to3(a)).reshape(t, heads, RWKV_HEAD)
    mean = jnp.mean(y, axis=-1, keepdims=True)
    var = jnp.mean(jnp.square(y - mean), axis=-1, keepdims=True)
    y = ((y - mean) * lax.rsqrt(var + GN_EPS)).reshape(t, d) * gn_w + gn_b
    rh, kh, vh = (u.reshape(t, heads, RWKV_HEAD) for u in (r, k, v))
    bonus = (jnp.sum(rh * kh * r_k, axis=-1, keepdims=True) * vh).reshape(t, d)
    out = _matmul((y + bonus) * g, w_o)
    return out, v_first


def kernel(x, meta_tokens, mix_norm_g, ffn_norm_g, ffn_w_gate, ffn_w_up, ffn_w_down, rwkv_mu, rwkv_w_r, rwkv_w_k, rwkv_w_v, rwkv_w_o, rwkv_dec_w0, rwkv_dec_w1, rwkv_dec_w2, rwkv_a_w0, rwkv_a_w1, rwkv_a_w2, rwkv_g_w1, rwkv_g_w2, rwkv_k_k, rwkv_k_a, rwkv_r_k, rwkv_gn_w, rwkv_gn_b, rwkv_v_w0, rwkv_v_w1, rwkv_v_w2, kv_norm_g, sb_w_k, sb_w_v, sb_k_gain, sb_w_q, sb_q_gain, sb_w_o):
    bsz, seq, d = x.shape
    depth = mix_norm_g.shape[0]
    n_a = rwkv_mu.shape[0]
    front_pad = (-N_META) % BLOCK
    p_rows = front_pad + N_META + seq
    assert p_rows % BLOCK == 0 and p_rows % RWKV_CHUNK == 0
    t = bsz * p_rows

    meta = jnp.broadcast_to(meta_tokens[None].astype(x.dtype), (bsz, N_META, d))
    h = jnp.concatenate([jnp.zeros((bsz, front_pad, d), x.dtype), meta, x], axis=1).reshape(t, d)
    row_ok = (jnp.arange(p_rows) >= front_pad)[None, :, None]

    def add_masked(h2, mix2):
        return jnp.where(row_ok, (h2 + mix2).reshape(bsz, p_rows, d), 0.0).reshape(t, d)

    v_first = None
    k_sh = v_sh = None
    for layer in range(depth):
        hn = _rms(h, mix_norm_g[layer])
        if layer < n_a:
            i = layer
            vres = None if i == 0 else (rwkv_v_w0[i - 1], rwkv_v_w1[i - 1], rwkv_v_w2[i - 1])
            mix, v_first = _rwkv_time_mix(
                hn.reshape(bsz, p_rows, d), v_first, rwkv_mu[i], rwkv_w_r[i], rwkv_w_k[i], rwkv_w_v[i],
                rwkv_w_o[i], rwkv_dec_w0[i], rwkv_dec_w1[i], rwkv_dec_w2[i], rwkv_a_w0[i], rwkv_a_w1[i],
                rwkv_a_w2[i], rwkv_g_w1[i], rwkv_g_w2[i], rwkv_k_k[i], rwkv_k_a[i],
                rwkv_r_k[i].reshape(1, d // RWKV_HEAD, RWKV_HEAD), rwkv_gn_w[i], rwkv_gn_b[i], vres)
        else:
            j = layer - n_a
            if j == 0:
                kvn = _rms(h, kv_norm_g)
                k_sh = _head_rms(_matmul(kvn, sb_w_k), sb_k_gain, SB_HEAD).astype(BF16).reshape(bsz, p_rows, d)
                v_sh = _matmul(kvn, sb_w_v, BF16).reshape(bsz, p_rows, d)
            q = _head_rms(_matmul(hn, sb_w_q[j]), sb_q_gain[j], SB_HEAD).astype(BF16)
            o = _sb_attention(q.reshape(bsz, p_rows, d), k_sh, v_sh, front_pad=front_pad)
            mix = _matmul(o.reshape(t, d), sb_w_o[j])
        h = add_masked(h, mix)
        h = _ffn(h, ffn_norm_g[layer], ffn_w_gate[layer], ffn_w_up[layer], ffn_w_down[layer],
                 seq_rows=p_rows, front_pad=front_pad)
    return h.reshape(bsz, p_rows, d)[:, front_pad + N_META:]
```

```python
import functools

import jax
import jax.numpy as jnp
from jax import lax
from jax.experimental import pallas as pl
from jax.experimental.pallas import tpu as pltpu

N_META = 16
BLOCK = 128
RWKV_HEAD = 64
SB_HEAD = 128
RMS_EPS = 1e-6
GN_EPS = 64e-5

LANES = 128
RWKV_CHUNK = 64
RWKV_PAIRS_PER_STEP = 8
SB_BLOCK_ROWS = 384
SB_HEADS_PER_STEP = 2
LOG2E = 1.4426950408889634
VMEM_LIMIT_BYTES = 56 * 1024 * 1024

F32 = jnp.float32
BF16 = jnp.bfloat16


def _largest_tile(n, cap, quantum):
    best = None
    for t in range(quantum, min(n, cap) + 1, quantum):
        if n % t == 0:
            best = t
    assert best is not None, (n, cap, quantum)
    return best


def _params(*sem):
    return pltpu.CompilerParams(dimension_semantics=sem, vmem_limit_bytes=VMEM_LIMIT_BYTES)


def _mm_kernel(x_ref, w_ref, o_ref):
    o_ref[...] = jnp.dot(x_ref[...], w_ref[...], preferred_element_type=F32).astype(o_ref.dtype)


def _matmul(x, w, out_dtype=F32):
    m, k = x.shape
    n = w.shape[1]
    tm = _largest_tile(m, 1056, 16)
    tn = _largest_tile(n, 1024, LANES)
    return pl.pallas_call(
        _mm_kernel,
        out_shape=jax.ShapeDtypeStruct((m, n), out_dtype),
        grid=(n // tn, m // tm),
        in_specs=[pl.BlockSpec((tm, k), lambda j, i: (i, 0)),
                  pl.BlockSpec((k, tn), lambda j, i: (0, j))],
        out_specs=pl.BlockSpec((tm, tn), lambda j, i: (i, j)),
        compiler_params=_params("parallel", "parallel"),
        name="matmul",
    )(x.astype(BF16), w.astype(BF16))


def _ffn_kernel(h_ref, g_ref, wg_ref, wu_ref, wd_ref, o_ref, xn_ref, *, tiles_per_seq, front_pad):
    f = pl.program_id(1)

    @pl.when(f == 0)
    def _():
        h = h_ref[...]
        ms = jnp.mean(h * h, axis=-1, keepdims=True)
        xn_ref[...] = (h * lax.rsqrt(ms + RMS_EPS) * g_ref[...]).astype(BF16)
        o_ref[...] = h

    xn = xn_ref[...]
    gate = jnp.dot(xn, wg_ref[...], preferred_element_type=F32)
    up = jnp.dot(xn, wu_ref[...], preferred_element_type=F32)
    act = (gate * jax.nn.sigmoid(gate) * up).astype(BF16)
    o_ref[...] += jnp.dot(act, wd_ref[...], preferred_element_type=F32)

    @pl.when(f == pl.num_programs(1) - 1)
    def _():
        tm = o_ref.shape[0]
        row = lax.rem(pl.program_id(0), tiles_per_seq) * tm + lax.broadcasted_iota(jnp.int32, (tm, 1), 0)
        o_ref[...] = jnp.where(row >= front_pad, o_ref[...], 0.0)


def _ffn(h, g, wg, wu, wd, *, seq_rows, front_pad):
    m, d = h.shape
    dff = wg.shape[1]
    tm = _largest_tile(seq_rows, 528, 16)
    tf = _largest_tile(dff, 512, LANES)
    kern = functools.partial(_ffn_kernel, tiles_per_seq=seq_rows // tm, front_pad=front_pad)
    return pl.pallas_call(
        kern,
        out_shape=jax.ShapeDtypeStruct((m, d), F32),
        grid=(m // tm, dff // tf),
        in_specs=[pl.BlockSpec((tm, d), lambda i, f: (i, 0)),
                  pl.BlockSpec((1, d), lambda i, f: (0, 0)),
                  pl.BlockSpec((d, tf), lambda i, f: (0, f)),
                  pl.BlockSpec((d, tf), lambda i, f: (0, f)),
                  pl.BlockSpec((tf, d), lambda i, f: (f, 0))],
        out_specs=pl.BlockSpec((tm, d), lambda i, f: (i, 0)),
        scratch_shapes=[pltpu.VMEM((tm, d), BF16)],
        compiler_params=_params("parallel", "arbitrary"),
        name="swiglu_block",
    )(h, g.reshape(1, d), wg.astype(BF16), wu.astype(BF16), wd.astype(BF16))


def _split3(x):
    hi = x.astype(BF16)
    r1 = x - hi.astype(F32)
    mid = r1.astype(BF16)
    lo = (r1 - mid.astype(F32)).astype(BF16)
    return hi, mid, lo


def _rwkv_group_chunk(inputs, states, consts):
    lane_h0, l_incl, m_strict, m_incl, eye, level_masks = consts
    pairs = range(len(inputs))
    c = inputs[0][0].shape[0]
    c2 = 2 * c

    def stack(x):
        return jnp.concatenate([jnp.where(lane_h0, x, 0.0), jnp.where(lane_h0, 0.0, x)], axis=0)

    lws, cums = [], []
    for r, wl, k, v, kk, al in inputs:
        lw = -jnp.exp(wl)
        hi, mid, lo = _split3(lw)
        cums.append(jnp.dot(l_incl, hi, preferred_element_type=F32)
                    + jnp.dot(l_incl, mid, preferred_element_type=F32)
                    + jnp.dot(l_incl, lo, preferred_element_type=F32))
        lws.append(lw)

    a_s, r_s, v_s, bk_hat, bk_til, decay_all = [], [], [], [], [], []
    for p in pairs:
        r, wl, k, v, kk, al = inputs[p]
        cum, lw = cums[p], lws[p]
        total = cum[c - 1:c, :]
        e_pos = jnp.exp(cum)
        e_prev = jnp.exp(cum - lw)
        e_neg = jnp.exp(-cum)
        e_rem = jnp.exp(total - cum)
        b = kk * al
        a_s.append(stack(-kk * e_prev).astype(BF16))
        r_s.append(stack(r * e_pos).astype(BF16))
        v_s.append(stack(v).astype(BF16))
        bk_hat.append(jnp.concatenate([stack(b * e_rem), stack(k * e_rem)], axis=0).astype(BF16))
        bt = (b * e_neg).astype(BF16)
        kt = (k * e_neg).astype(BF16)
        bk_til.append(jnp.concatenate([bt, bt, kt, kt], axis=0))
        decay_all.append(jnp.exp(total))

    s_bf = [states[p].astype(BF16) for p in pairs]
    nt = lambda x, y: lax.dot_general(x, y, (((1,), (1,)), ((), ())), preferred_element_type=F32)
    nn = lambda x, y: jnp.dot(x, y, preferred_element_type=F32)

    gram = [nt(jnp.concatenate([a_s[p], r_s[p]], axis=0), bk_til[p]) for p in pairs]
    n = [jnp.where(m_strict, gram[p][:c2, :c2], 0.0) for p in pairs]
    n_bf = [x.astype(BF16) for x in n]
    a_ak = [jnp.where(m_strict, gram[p][:c2, c2:], 0.0).astype(BF16) for p in pairs]
    a_rb = [jnp.where(m_incl, gram[p][c2:, :c2], 0.0).astype(BF16) for p in pairs]
    a_rk = [jnp.where(m_incl, gram[p][c2:, c2:], 0.0).astype(BF16) for p in pairs]

    zero_bf = jnp.zeros((), BF16)
    n0 = [jnp.where(level_masks[0], n[p], 0.0) for p in pairs]
    n0_bf = [x.astype(BF16) for x in n0]
    p2 = [nn(n0_bf[p], n0_bf[p]) for p in pairs]
    rhs = [nt(a_s[p], s_bf[p]) + nn(a_ak[p], v_s[p]) for p in pairs]
    p2_bf = [x.astype(BF16) for x in p2]
    p4 = [nn(p2_bf[p], p2_bf[p]) for p in pairs]
    x = [eye + n0[p] + p2[p] + nn(n0_bf[p], p2_bf[p]) for p in pairs]
    y_part = [nt(r_s[p], s_bf[p]) + nn(a_rk[p], v_s[p]) for p in pairs]
    x = [x[p] + nn(x[p].astype(BF16), p4[p].astype(BF16)) for p in pairs]
    for off_mask in level_masks[1:]:
        x_bf = [x[p].astype(BF16) for p in pairs]
        t = [nn(x_bf[p], jnp.where(off_mask, n_bf[p], zero_bf)) for p in pairs]
        x = [x[p] + nn(t[p].astype(BF16), x_bf[p]) for p in pairs]

    sa = [nn(x[p].astype(BF16), rhs[p].astype(BF16)).astype(BF16) for p in pairs]
    ys, s_new = [], []
    for p in pairs:
        y_s = y_part[p] + nn(a_rb[p], sa[p])
        ys.append(y_s[:c, :] + y_s[c:, :])
    for p in pairs:
        lhs_t = jnp.concatenate([sa[p], v_s[p]], axis=0).astype(F32).T.astype(BF16)
        s_new.append(states[p] * decay_all[p] + nn(lhs_t, bk_hat[p]))
    return ys, s_new


def _rwkv_consts(c):
    c2 = 2 * c
    row = lax.broadcasted_iota(jnp.int32, (c2, c2), 0)
    col = lax.broadcasted_iota(jnp.int32, (c2, c2), 1)

    def same(size):
        sh = size.bit_length() - 1
        return lax.shift_right_logical(row, sh) == lax.shift_right_logical(col, sh)

    same_head = same(c)
    m_strict = same_head & (col < row)
    m_incl = same_head & (col <= row)
    eye = jnp.where(row == col, 1.0, 0.0).astype(F32)
    levels = [same(8)]
    size = 16
    while size <= c:
        levels.append(same(size) & jnp.logical_not(same(size // 2)))
        size *= 2
    lane_h0 = lax.broadcasted_iota(jnp.int32, (c, LANES), 1) < RWKV_HEAD
    rr = lax.broadcasted_iota(jnp.int32, (c, c), 0)
    cc = lax.broadcasted_iota(jnp.int32, (c, c), 1)
    l_incl = jnp.where(cc <= rr, 1.0, 0.0).astype(BF16)
    return lane_h0, l_incl, m_strict, m_incl, eye, levels


def _rwkv_kernel(r_ref, w_ref, k_ref, v_ref, kk_ref, a_ref, y_ref, s_ref, *, pairs):
    @pl.when(pl.program_id(2) == 0)
    def _():
        s_ref[...] = jnp.zeros_like(s_ref)

    consts = _rwkv_consts(r_ref.shape[1])
    refs = (r_ref, w_ref, k_ref, v_ref, kk_ref, a_ref)
    lanes = [slice(p * LANES, (p + 1) * LANES) for p in range(pairs)]
    inputs = [tuple(ref[0, :, sl] for ref in refs) for sl in lanes]
    ys, s_new = _rwkv_group_chunk(inputs, [s_ref[p] for p in range(pairs)], consts)
    for p in range(pairs):
        y_ref[0, :, lanes[p]] = ys[p]
        s_ref[p] = s_new[p]


def _rwkv_scan(r, wl, k, v, kk, al):
    bsz, p_rows, d = r.shape
    c = RWKV_CHUNK
    pairs = _largest_tile(d // LANES, RWKV_PAIRS_PER_STEP, 1)
    width = pairs * LANES
    spec = pl.BlockSpec((1, c, width), lambda b, g, t: (b, t, g))
    return pl.pallas_call(
        functools.partial(_rwkv_kernel, pairs=pairs),
        out_shape=jax.ShapeDtypeStruct((bsz, p_rows, d), F32),
        grid=(bsz, d // width, p_rows // c),
        in_specs=[spec] * 6,
        out_specs=spec,
        scratch_shapes=[pltpu.VMEM((pairs, LANES, LANES), F32)],
        compiler_params=_params("parallel", "parallel", "arbitrary"),
        name="rwkv7_scan",
    )(r, wl, k, v, kk, al)


def _sb_block(j, q, k_ref, v_ref, acc_ref, later_ref, suffix_ones, heads, *, causal, pad):
    qb = q[0].shape[0]
    nsub = qb // LANES
    start = pl.multiple_of(j * qb, qb)
    row = lax.broadcasted_iota(jnp.int32, (qb, LANES), 0)
    lane = lax.broadcasted_iota(jnp.int32, (qb, LANES), 1)

    def valid(c):
        col = lane + c * LANES
        use_pad = pad is not None and c * LANES < pad
        if causal and use_pad:
            return (col < row) & (col >= pad)
        if causal:
            return col < row
        if use_pad:
            return col >= pad
        return None

    zl = []
    for h in heads:
        kb = k_ref[0, pl.ds(start, qb), h * LANES:(h + 1) * LANES]
        zl.append(lax.dot_general(q[h], kb, (((1,), (1,)), ((), ())), preferred_element_type=F32))
    base, cs = [], []
    for h in heads:
        base_h, cs_h = [], []
        for c in range(nsub):
            z = zl[h][:, c * LANES:(c + 1) * LANES]
            neg_abs = pltpu.bitcast(pltpu.bitcast(z, jnp.uint32) | jnp.uint32(0x80000000), F32)
            s = jnp.maximum(z, 0.0) + jnp.log2(1.0 + jnp.exp2(neg_abs))
            m = valid(c)
            if m is not None:
                s = jnp.where(m, s, 0.0)
            s_hi = s.astype(BF16)
            s_lo = (s - s_hi.astype(F32)).astype(BF16)
            cs_h.append(jnp.dot(s_hi, suffix_ones, preferred_element_type=F32)
                        + jnp.dot(s_lo, suffix_ones, preferred_element_type=F32))
            base_h.append(z - s)
        base.append(base_h)
        cs.append(cs_h)
    w = []
    for h in heads:
        running = later_ref[h]
        parts = [None] * nsub
        for c in reversed(range(nsub)):
            wc = jnp.exp2(base[h][c] - (cs[h][c][:, :LANES] + running))
            m = valid(c)
            if m is not None:
                wc = jnp.where(m, wc, 0.0)
            parts[c] = wc.astype(BF16)
            running = running + cs[h][c][:, LANES:]
        later_ref[h] = running
        w.append(jnp.concatenate(parts, axis=1))
    for h in heads:
        vb = v_ref[0, pl.ds(start, qb), h * LANES:(h + 1) * LANES]
        acc_ref[h] += jnp.dot(w[h], vb, preferred_element_type=F32)


def _sb_kernel(q_ref, k_ref, v_ref, o_ref, acc_ref, later_ref, *, front_pad, n_heads):
    i = pl.program_id(2)
    heads = range(n_heads)
    q = [q_ref[0, :, h * LANES:(h + 1) * LANES] for h in heads]
    rr = lax.broadcasted_iota(jnp.int32, (LANES, 2 * LANES), 0)
    cc = lax.broadcasted_iota(jnp.int32, (LANES, 2 * LANES), 1)
    suffix_ones = jnp.where((cc >= LANES) | (rr > cc), 1.0, 0.0).astype(BF16)
    acc_ref[...] = jnp.zeros_like(acc_ref)
    later_ref[...] = jnp.zeros_like(later_ref)
    block = functools.partial(_sb_block, q=q, k_ref=k_ref, v_ref=v_ref, acc_ref=acc_ref,
                              later_ref=later_ref, suffix_ones=suffix_ones, heads=heads)

    @pl.when(i == 0)
    def _():
        block(0, causal=True, pad=front_pad)

    @pl.when(i > 0)
    def _():
        block(i, causal=True, pad=None)

        def interior(jj, carry):
            block(i - jj, causal=False, pad=None)
            return carry

        lax.fori_loop(1, i, interior, 0)
        block(0, causal=False, pad=front_pad)

    for h in heads:
        o_ref[0, :, h * LANES:(h + 1) * LANES] = acc_ref[h].astype(o_ref.dtype)


def _sb_attention(q, k, v, *, front_pad):
    bsz, p_rows, d = q.shape
    qb = _largest_tile(p_rows, SB_BLOCK_ROWS, LANES)
    assert front_pad <= qb
    n_heads = _largest_tile(d // SB_HEAD, SB_HEADS_PER_STEP, 1)
    width = n_heads * SB_HEAD
    kern = functools.partial(_sb_kernel, front_pad=front_pad, n_heads=n_heads)
    kv_spec = pl.BlockSpec((1, p_rows, width), lambda b, h, i: (b, 0, h))
    q_spec = pl.BlockSpec((1, qb, width), lambda b, h, i: (b, i, h))
    return pl.pallas_call(
        kern,
        out_shape=jax.ShapeDtypeStruct((bsz, p_rows, d), BF16),
        grid=(bsz, d // width, p_rows // qb),
        in_specs=[q_spec, kv_spec, kv_spec],
        out_specs=q_spec,
        scratch_shapes=[pltpu.VMEM((n_heads, qb, SB_HEAD), F32), pltpu.VMEM((n_heads, qb, SB_HEAD), F32)],
        compiler_params=_params("parallel", "parallel", "arbitrary"),
        name="stick_breaking_attention",
    )(q, k, v)


def _rms(x, g):
    return x * lax.rsqrt(jnp.mean(x * x, axis=-1, keepdims=True) + RMS_EPS) * g


def _head_rms(x, g, head):
    shp = x.shape
    xh = x.reshape(shp[:-1] + (shp[-1] // head, head))
    return _rms(xh, g).reshape(shp)


def _pad_cols(w, n):
    return jnp.pad(w, ((0, 0), (0, n - w.shape[1])))


def _pad_rows(w, n):
    return jnp.pad(w, ((0, n - w.shape[0]), (0, 0)))


def _lora(x, w1, w2, act):
    rank = w1.shape[1]
    rp = -(-rank // LANES) * LANES
    mid = _matmul(x, _pad_cols(w1, rp))
    return _matmul(act(mid), _pad_rows(w2, rp))


def _rwkv_time_mix(hn3, v_first, mu, w_r, w_k, w_v, w_o, dec_w0, dec_w1, dec_w2, a_w0, a_w1, a_w2,
                   g_w1, g_w2, k_k, k_a, r_k, gn_w, gn_b, vres):
    bsz, p_rows, d = hn3.shape
    t = bsz * p_rows
    xx = jnp.pad(hn3, ((0, 0), (1, 0), (0, 0)))[:, :-1] - hn3
    xr, xw, xk, xv, xa, xg = ((hn3 + xx * mu[i]).reshape(t, d).astype(BF16) for i in range(6))
    r = _matmul(xr, w_r)
    k = _matmul(xk, w_k)
    v = _matmul(xv, w_v)
    wl = -jax.nn.softplus(-(dec_w0 + _lora(xw, dec_w1, dec_w2, jnp.tanh))) - 0.5
    a = jax.nn.sigmoid(a_w0 + _lora(xa, a_w1, a_w2, lambda u: u))
    g = _lora(xg, g_w1, g_w2, jax.nn.sigmoid)
    if vres is None:
        v_first = v
    else:
        v_w0, v_w1, v_w2 = vres
        v = v + (v_first - v) * jax.nn.sigmoid(v_w0 + _lora(xv, v_w1, v_w2, lambda u: u))
    heads = d // RWKV_HEAD
    kk = (k * k_k).reshape(t, heads, RWKV_HEAD)
    kk = (kk * lax.rsqrt(jnp.maximum(jnp.sum(kk * kk, axis=-1, keepdims=True), 1e-24))).reshape(t, d)
    k = k * (1.0 + (a - 1.0) * k_a)
    to3 = lambda u: u.reshape(bsz, p_rows, d)
    y = _rwkv_scan(to3(r), to3(wl), to3(k), to3(v), to3(kk), to3(a)).reshape(t, heads, RWKV_HEAD)
    mean = jnp.mean(y, axis=-1, keepdims=True)
    var = jnp.mean(jnp.square(y - mean), axis=-1, keepdims=True)
    y = ((y - mean) * lax.rsqrt(var + GN_EPS)).reshape(t, d) * gn_w + gn_b
    rh, kh, vh = (u.reshape(t, heads, RWKV_HEAD) for u in (r, k, v))
    bonus = (jnp.sum(rh * kh * r_k, axis=-1, keepdims=True) * vh).reshape(t, d)
    out = _matmul((y + bonus) * g, w_o)
    return out, v_first


def kernel(x, meta_tokens, mix_norm_g, ffn_norm_g, ffn_w_gate, ffn_w_up, ffn_w_down, rwkv_mu, rwkv_w_r, rwkv_w_k, rwkv_w_v, rwkv_w_o, rwkv_dec_w0, rwkv_dec_w1, rwkv_dec_w2, rwkv_a_w0, rwkv_a_w1, rwkv_a_w2, rwkv_g_w1, rwkv_g_w2, rwkv_k_k, rwkv_k_a, rwkv_r_k, rwkv_gn_w, rwkv_gn_b, rwkv_v_w0, rwkv_v_w1, rwkv_v_w2, kv_norm_g, sb_w_k, sb_w_v, sb_k_gain, sb_w_q, sb_q_gain, sb_w_o):
    bsz, seq, d = x.shape
    depth = mix_norm_g.shape[0]
    n_a = rwkv_mu.shape[0]
    front_pad = (-N_META) % BLOCK
    p_rows = front_pad + N_META + seq
    assert p_rows % BLOCK == 0 and p_rows % RWKV_CHUNK == 0
    t = bsz * p_rows

    meta = jnp.broadcast_to(meta_tokens[None].astype(x.dtype), (bsz, N_META, d))
    h = jnp.concatenate([jnp.zeros((bsz, front_pad, d), x.dtype), meta, x], axis=1).reshape(t, d)
    row_ok = (jnp.arange(p_rows) >= front_pad)[None, :, None]

    def add_masked(h2, mix2):
        return jnp.where(row_ok, (h2 + mix2).reshape(bsz, p_rows, d), 0.0).reshape(t, d)

    v_first = None
    k_sh = v_sh = None
    for layer in range(depth):
        hn = _rms(h, mix_norm_g[layer])
        if layer < n_a:
            i = layer
            vres = None if i == 0 else (rwkv_v_w0[i - 1], rwkv_v_w1[i - 1], rwkv_v_w2[i - 1])
            mix, v_first = _rwkv_time_mix(
                hn.reshape(bsz, p_rows, d), v_first, rwkv_mu[i], rwkv_w_r[i], rwkv_w_k[i], rwkv_w_v[i],
                rwkv_w_o[i], rwkv_dec_w0[i], rwkv_dec_w1[i], rwkv_dec_w2[i], rwkv_a_w0[i], rwkv_a_w1[i],
                rwkv_a_w2[i], rwkv_g_w1[i], rwkv_g_w2[i], rwkv_k_k[i], rwkv_k_a[i],
                rwkv_r_k[i].reshape(1, d // RWKV_HEAD, RWKV_HEAD), rwkv_gn_w[i], rwkv_gn_b[i], vres)
        else:
            j = layer - n_a
            if j == 0:
                kvn = _rms(h, kv_norm_g)
                k_sh = _head_rms(_matmul(kvn, sb_w_k), sb_k_gain, SB_HEAD).astype(BF16).reshape(bsz, p_rows, d)
                v_sh = _matmul(kvn, sb_w_v, BF16).reshape(bsz, p_rows, d)
            q = (_head_rms(_matmul(hn, sb_w_q[j]), sb_q_gain[j], SB_HEAD) * (SB_HEAD ** -0.5 * LOG2E)).astype(BF16)
            o = _sb_attention(q.reshape(bsz, p_rows, d), k_sh, v_sh, front_pad=front_pad)
            mix = _matmul(o.reshape(t, d), sb_w_o[j])
        h = add_masked(h, mix)
        h = _ffn(h, ffn_norm_g[layer], ffn_w_gate[layer], ffn_w_up[layer], ffn_w_down[layer],
                 seq_rows=p_rows, front_pad=front_pad)
    return h.reshape(bsz, p_rows, d)[:, front_pad + N_META:]
```

```python
import functools

import jax
import jax.numpy as jnp
from jax import lax
from jax.experimental import pallas as pl
from jax.experimental.pallas import tpu as pltpu

N_META = 16
BLOCK = 128
RWKV_HEAD = 64
SB_HEAD = 128
RMS_EPS = 1e-6
GN_EPS = 64e-5

LANES = 128
RWKV_CHUNK = 64
RWKV_PAIRS_PER_STEP = 8
SB_BLOCK_ROWS = 384
SB_HEADS_PER_STEP = 2
LOG2E = 1.4426950408889634
VMEM_LIMIT_BYTES = 56 * 1024 * 1024

F32 = jnp.float32
BF16 = jnp.bfloat16


def _largest_tile(n, cap, quantum):
    best = None
    for t in range(quantum, min(n, cap) + 1, quantum):
        if n % t == 0:
            best = t
    assert best is not None, (n, cap, quantum)
    return best


def _params(*sem):
    return pltpu.CompilerParams(dimension_semantics=sem, vmem_limit_bytes=VMEM_LIMIT_BYTES)


def _rms_rows(x, g):
    return x * lax.rsqrt(jnp.mean(x * x, axis=-1, keepdims=True) + RMS_EPS) * g


def _softplus(x):
    return jnp.maximum(x, 0.0) + jnp.log(1.0 + jnp.exp(-jnp.abs(x)))


def _mm_kernel(x_ref, w_ref, o_ref):
    o_ref[...] = jnp.dot(x_ref[...], w_ref[...], preferred_element_type=F32).astype(o_ref.dtype)


def _mm_head_rms_kernel(x_ref, w_ref, gain_ref, o_ref, *, scale):
    acc = jnp.dot(x_ref[...], w_ref[...], preferred_element_type=F32)
    gain = gain_ref[...] * scale
    for c in range(acc.shape[1] // SB_HEAD):
        sl = slice(c * SB_HEAD, (c + 1) * SB_HEAD)
        o_ref[:, sl] = _rms_rows(acc[:, sl], gain).astype(o_ref.dtype)


def _matmul(x, w, out_dtype=F32, head_gain=None, head_scale=1.0):
    m, k = x.shape
    n = w.shape[1]
    tm = _largest_tile(m, 1056, 16)
    tn = _largest_tile(n, 1024, LANES)
    in_specs = [pl.BlockSpec((tm, k), lambda j, i: (i, 0)),
                pl.BlockSpec((k, tn), lambda j, i: (0, j))]
    args = [x.astype(BF16), w.astype(BF16)]
    if head_gain is None:
        kern = _mm_kernel
    else:
        kern = functools.partial(_mm_head_rms_kernel, scale=head_scale)
        in_specs.append(pl.BlockSpec((1, SB_HEAD), lambda j, i: (0, 0)))
        args.append(head_gain.reshape(1, SB_HEAD))
    return pl.pallas_call(
        kern,
        out_shape=jax.ShapeDtypeStruct((m, n), out_dtype),
        grid=(n // tn, m // tm),
        in_specs=in_specs,
        out_specs=pl.BlockSpec((tm, tn), lambda j, i: (i, j)),
        compiler_params=_params("parallel", "parallel"),
        name="matmul",
    )(*args)


def _lora_kernel(x_ref, w1_ref, w2_ref, o_ref, *, act):
    mid = jnp.dot(x_ref[...], w1_ref[...], preferred_element_type=F32)
    if act == "tanh":
        mid = jnp.tanh(mid)
    elif act == "sigmoid":
        mid = jax.nn.sigmoid(mid)
    o_ref[...] = jnp.dot(mid.astype(BF16), w2_ref[...], preferred_element_type=F32)


def _lora(x, w1, w2, act):
    m, k = x.shape
    n = w2.shape[1]
    rank = w1.shape[1]
    rp = -(-rank // LANES) * LANES
    w1p = jnp.pad(w1, ((0, 0), (0, rp - rank))).astype(BF16)
    w2p = jnp.pad(w2, ((0, rp - rank), (0, 0))).astype(BF16)
    tm = _largest_tile(m, 1056, 16)
    return pl.pallas_call(
        functools.partial(_lora_kernel, act=act),
        out_shape=jax.ShapeDtypeStruct((m, n), F32),
        grid=(m // tm,),
        in_specs=[pl.BlockSpec((tm, k), lambda i: (i, 0)),
                  pl.BlockSpec((k, rp), lambda i: (0, 0)),
                  pl.BlockSpec((rp, n), lambda i: (0, 0))],
        out_specs=pl.BlockSpec((tm, n), lambda i: (i, 0)),
        compiler_params=_params("parallel"),
        name="lora",
    )(x, w1p, w2p)


def _rms_cast_kernel(h_ref, g_ref, o_ref):
    o_ref[...] = _rms_rows(h_ref[...], g_ref[...]).astype(o_ref.dtype)


def _rms_cast(h, g, *, seq_rows):
    m, d = h.shape
    tm = _largest_tile(seq_rows, 1056, 16)
    spec = pl.BlockSpec((tm, d), lambda i: (i, 0))
    return pl.pallas_call(
        _rms_cast_kernel,
        out_shape=jax.ShapeDtypeStruct((m, d), BF16),
        grid=(m // tm,),
        in_specs=[spec, pl.BlockSpec((1, d), lambda i: (0, 0))],
        out_specs=spec,
        compiler_params=_params("parallel"),
        name="rms_cast",
    )(h, g.reshape(1, d))


def _token_mix_kernel(h_ref, hp_ref, g_ref, mu_ref, *o_refs, tiles_per_seq, front_pad):
    g = g_ref[...]
    hn = _rms_rows(h_ref[...], g)
    xx = _rms_rows(hp_ref[...], g) - hn
    tm = hn.shape[0]
    row = lax.rem(pl.program_id(0), tiles_per_seq) * tm + lax.broadcasted_iota(jnp.int32, (tm, 1), 0)
    keep = row >= front_pad
    for i, o_ref in enumerate(o_refs):
        o_ref[...] = jnp.where(keep, hn + xx * mu_ref[i:i + 1, :], 0.0).astype(o_ref.dtype)


def _token_mix(h, h_prev, g, mu, *, seq_rows, front_pad):
    m, d = h.shape
    n_mix = mu.shape[0]
    tm = _largest_tile(seq_rows, 384, 16)
    spec = pl.BlockSpec((tm, d), lambda i: (i, 0))
    kern = functools.partial(_token_mix_kernel, tiles_per_seq=seq_rows // tm, front_pad=front_pad)
    return pl.pallas_call(
        kern,
        out_shape=[jax.ShapeDtypeStruct((m, d), BF16)] * n_mix,
        grid=(m // tm,),
        in_specs=[spec, spec, pl.BlockSpec((1, d), lambda i: (0, 0)),
                  pl.BlockSpec((n_mix, d), lambda i: (0, 0))],
        out_specs=[spec] * n_mix,
        compiler_params=_params("parallel"),
        name="token_mix",
    )(h, h_prev, g.reshape(1, d), mu)


def _ffn_kernel(h_ref, mix_ref, g_ref, wg_ref, wu_ref, wd_ref, o_ref, xn_ref, *, tiles_per_seq, front_pad):
    f = pl.program_id(1)

    @pl.when(f == 0)
    def _():
        h = h_ref[...] + mix_ref[...]
        xn_ref[...] = _rms_rows(h, g_ref[...]).astype(BF16)
        o_ref[...] = h

    xn = xn_ref[...]
    gate = jnp.dot(xn, wg_ref[...], preferred_element_type=F32)
    up = jnp.dot(xn, wu_ref[...], preferred_element_type=F32)
    act = (gate * jax.nn.sigmoid(gate) * up).astype(BF16)
    o_ref[...] += jnp.dot(act, wd_ref[...], preferred_element_type=F32)

    @pl.when(f == pl.num_programs(1) - 1)
    def _():
        tm = o_ref.shape[0]
        row = lax.rem(pl.program_id(0), tiles_per_seq) * tm + lax.broadcasted_iota(jnp.int32, (tm, 1), 0)
        o_ref[...] = jnp.where(row >= front_pad, o_ref[...], 0.0)


def _ffn(h, mix, g, wg, wu, wd, *, seq_rows, front_pad):
    m, d = h.shape
    dff = wg.shape[1]
    tm = _largest_tile(seq_rows, 528, 16)
    tf = _largest_tile(dff, 512, LANES)
    kern = functools.partial(_ffn_kernel, tiles_per_seq=seq_rows // tm, front_pad=front_pad)
    row_spec = pl.BlockSpec((tm, d), lambda i, f: (i, 0))
    return pl.pallas_call(
        kern,
        out_shape=jax.ShapeDtypeStruct((m, d), F32),
        grid=(m // tm, dff // tf),
        in_specs=[row_spec, row_spec,
                  pl.BlockSpec((1, d), lambda i, f: (0, 0)),
                  pl.BlockSpec((d, tf), lambda i, f: (0, f)),
                  pl.BlockSpec((d, tf), lambda i, f: (0, f)),
                  pl.BlockSpec((tf, d), lambda i, f: (f, 0))],
        out_specs=row_spec,
        scratch_shapes=[pltpu.VMEM((tm, d), BF16)],
        compiler_params=_params("parallel", "arbitrary"),
        name="swiglu_block",
    )(h, mix, g.reshape(1, d), wg.astype(BF16), wu.astype(BF16), wd.astype(BF16))


def _split2(x):
    hi = x.astype(BF16)
    return hi, (x - hi.astype(F32)).astype(BF16)


def _split3(x):
    hi = x.astype(BF16)
    r1 = x - hi.astype(F32)
    mid = r1.astype(BF16)
    lo = (r1 - mid.astype(F32)).astype(BF16)
    return hi, mid, lo


def _rwkv_group_chunk(inputs, vecs, states, consts, use_vres):
    lane_h0, l_incl, m_strict, m_incl, eye, level_masks, head_ones = consts
    pairs = range(len(inputs))
    c = inputs[0][0].shape[0]
    c2 = 2 * c
    nt = lambda x, y: lax.dot_general(x, y, (((1,), (1,)), ((), ())), preferred_element_type=F32)
    nn = lambda x, y: jnp.dot(x, y, preferred_element_type=F32)

    def head_sum(x):
        hi, lo = _split2(x)
        return nn(hi, head_ones) + nn(lo, head_ones)

    def stack(x):
        return jnp.concatenate([jnp.where(lane_h0, x, 0.0), jnp.where(lane_h0, 0.0, x)], axis=0)

    lws, cums, kk_raw, kk_ss = [], [], [], []
    for p in pairs:
        dec = inputs[p][3]
        dec_w0, k_k = vecs[p][0], vecs[p][2]
        wl = -_softplus(-(dec_w0 + dec)) - 0.5
        lw = -jnp.exp(wl)
        hi, mid, lo = _split3(lw)
        cums.append(nn(l_incl, hi) + nn(l_incl, mid) + nn(l_incl, lo))
        lws.append(lw)
        kk0 = inputs[p][1] * k_k
        kk_raw.append(kk0)
        kk_ss.append(head_sum(kk0 * kk0))

    a_s, r_s, v_s, bk_hat, bk_til, decay_all, v_mix, rk = [], [], [], [], [], [], [], []
    for p in pairs:
        r, k, v = inputs[p][0], inputs[p][1], inputs[p][2]
        _, a_w0, _, k_a, r_k = vecs[p][:5]
        al = jax.nn.sigmoid(a_w0 + inputs[p][4])
        kk = kk_raw[p] * lax.rsqrt(jnp.maximum(kk_ss[p], 1e-24))
        k = k * (1.0 + (al - 1.0) * k_a)
        if use_vres:
            v = v + (inputs[p][7] - v) * jax.nn.sigmoid(vecs[p][7] + inputs[p][6])
        v_mix.append(v)
        rk.append(r * k * r_k)
        cum, lw = cums[p], lws[p]
        total = cum[c - 1:c, :]
        e_pos = jnp.exp(cum)
        e_prev = jnp.exp(cum - lw)
        e_neg = jnp.exp(-cum)
        e_rem = jnp.exp(total - cum)
        b = kk * al
        a_s.append(stack(-kk * e_prev).astype(BF16))
        r_s.append(stack(r * e_pos).astype(BF16))
        v_s.append(stack(v).astype(BF16))
        bk_hat.append(jnp.concatenate([stack(b * e_rem), stack(k * e_rem)], axis=0).astype(BF16))
        bt = (b * e_neg).astype(BF16)
        kt = (k * e_neg).astype(BF16)
        bk_til.append(jnp.concatenate([bt, bt, kt, kt], axis=0))
        decay_all.append(jnp.exp(total))

    s_bf = [states[p].astype(BF16) for p in pairs]
    gram = [nt(jnp.concatenate([a_s[p], r_s[p]], axis=0), bk_til[p]) for p in pairs]
    n = [jnp.where(m_strict, gram[p][:c2, :c2], 0.0) for p in pairs]
    n_bf = [x.astype(BF16) for x in n]
    a_ak = [jnp.where(m_strict, gram[p][:c2, c2:], 0.0).astype(BF16) for p in pairs]
    a_rb = [jnp.where(m_incl, gram[p][c2:, :c2], 0.0).astype(BF16) for p in pairs]
    a_rk = [jnp.where(m_incl, gram[p][c2:, c2:], 0.0).astype(BF16) for p in pairs]

    zero_bf = jnp.zeros((), BF16)
    n0 = [jnp.where(level_masks[0], n[p], 0.0) for p in pairs]
    n0_bf = [x.astype(BF16) for x in n0]
    p2 = [nn(n0_bf[p], n0_bf[p]) for p in pairs]
    rhs = [nt(a_s[p], s_bf[p]) + nn(a_ak[p], v_s[p]) for p in pairs]
    p2_bf = [x.astype(BF16) for x in p2]
    p4 = [nn(p2_bf[p], p2_bf[p]) for p in pairs]
    x = [eye + n0[p] + p2[p] + nn(n0_bf[p], p2_bf[p]) for p in pairs]
    y_part = [nt(r_s[p], s_bf[p]) + nn(a_rk[p], v_s[p]) for p in pairs]
    bonus_dot = [head_sum(rk[p]) for p in pairs]
    x = [x[p] + nn(x[p].astype(BF16), p4[p].astype(BF16)) for p in pairs]
    for off_mask in level_masks[1:]:
        x_bf = [x[p].astype(BF16) for p in pairs]
        t = [nn(x_bf[p], jnp.where(off_mask, n_bf[p], zero_bf)) for p in pairs]
        x = [x[p] + nn(t[p].astype(BF16), x_bf[p]) for p in pairs]

    sa = [nn(x[p].astype(BF16), rhs[p].astype(BF16)).astype(BF16) for p in pairs]
    ys, s_new = [], []
    for p in pairs:
        y_s = y_part[p] + nn(a_rb[p], sa[p])
        ys.append(y_s[:c, :] + y_s[c:, :])
    for p in pairs:
        lhs_t = jnp.concatenate([sa[p], v_s[p]], axis=0).astype(F32).T.astype(BF16)
        s_new.append(states[p] * decay_all[p] + nn(lhs_t, bk_hat[p]))

    inv_n = 1.0 / RWKV_HEAD
    means = [head_sum(ys[p]) * inv_n for p in pairs]
    cent = [ys[p] - means[p] for p in pairs]
    var = [head_sum(cent[p] * cent[p]) * inv_n for p in pairs]
    outs = []
    for p in pairs:
        gn_w, gn_b = vecs[p][5], vecs[p][6]
        yn = cent[p] * lax.rsqrt(var[p] + GN_EPS) * gn_w + gn_b
        outs.append(((yn + bonus_dot[p] * v_mix[p]) * inputs[p][5]).astype(BF16))
    return outs, s_new


def _rwkv_consts(c):
    c2 = 2 * c
    row = lax.broadcasted_iota(jnp.int32, (c2, c2), 0)
    col = lax.broadcasted_iota(jnp.int32, (c2, c2), 1)

    def same(size):
        sh = size.bit_length() - 1
        return lax.shift_right_logical(row, sh) == lax.shift_right_logical(col, sh)

    same_head = same(c)
    m_strict = same_head & (col < row)
    m_incl = same_head & (col <= row)
    eye = jnp.where(row == col, 1.0, 0.0).astype(F32)
    levels = [same(8)]
    size = 16
    while size <= c:
        levels.append(same(size) & jnp.logical_not(same(size // 2)))
        size *= 2
    lane_h0 = lax.broadcasted_iota(jnp.int32, (c, LANES), 1) < RWKV_HEAD
    rr = lax.broadcasted_iota(jnp.int32, (c, c), 0)
    cc = lax.broadcasted_iota(jnp.int32, (c, c), 1)
    l_incl = jnp.where(cc <= rr, 1.0, 0.0).astype(BF16)
    head_ones = jnp.where(same(RWKV_HEAD), 1.0, 0.0).astype(BF16)
    return lane_h0, l_incl, m_strict, m_incl, eye, levels, head_ones


def _rwkv_kernel(*refs, pairs, n_in, use_vres):
    in_refs, vec_ref, o_ref, s_ref = refs[:n_in], refs[n_in], refs[n_in + 1], refs[n_in + 2]

    @pl.when(pl.program_id(2) == 0)
    def _():
        s_ref[...] = jnp.zeros_like(s_ref)

    consts = _rwkv_consts(in_refs[0].shape[1])
    lanes = [slice(p * LANES, (p + 1) * LANES) for p in range(pairs)]
    inputs = [tuple(ref[0, :, sl] for ref in in_refs) for sl in lanes]
    vecs = [tuple(vec_ref[i:i + 1, sl] for i in range(vec_ref.shape[0])) for sl in lanes]
    outs, s_new = _rwkv_group_chunk(inputs, vecs, [s_ref[p] for p in range(pairs)], consts, use_vres)
    for p in range(pairs):
        o_ref[0, :, lanes[p]] = outs[p]
        s_ref[p] = s_new[p]


def _rwkv_core(tensors, vecs, use_vres):
    bsz, p_rows, d = tensors[0].shape
    c = RWKV_CHUNK
    pairs = _largest_tile(d // LANES, RWKV_PAIRS_PER_STEP, 1)
    width = pairs * LANES
    spec = pl.BlockSpec((1, c, width), lambda b, g, t: (b, t, g))
    n_in = len(tensors)
    return pl.pallas_call(
        functools.partial(_rwkv_kernel, pairs=pairs, n_in=n_in, use_vres=use_vres),
        out_shape=jax.ShapeDtypeStruct((bsz, p_rows, d), BF16),
        grid=(bsz, d // width, p_rows // c),
        in_specs=[spec] * n_in + [pl.BlockSpec((vecs.shape[0], width), lambda b, g, t: (0, g))],
        out_specs=spec,
        scratch_shapes=[pltpu.VMEM((pairs, LANES, LANES), F32)],
        compiler_params=_params("parallel", "parallel", "arbitrary"),
        name="rwkv7_scan",
    )(*tensors, vecs)


def _sb_block(j, q, k_ref, v_ref, acc_ref, later_ref, suffix_ones, heads, *, causal, pad):
    qb = q[0].shape[0]
    nsub = qb // LANES
    start = pl.multiple_of(j * qb, qb)
    row = lax.broadcasted_iota(jnp.int32, (qb, LANES), 0)
    lane = lax.broadcasted_iota(jnp.int32, (qb, LANES), 1)

    def valid(c):
        col = lane + c * LANES
        use_pad = pad is not None and c * LANES < pad
        if causal and use_pad:
            return (col < row) & (col >= pad)
        if causal:
            return col < row
        if use_pad:
            return col >= pad
        return None

    zl = []
    for h in heads:
        kb = k_ref[0, pl.ds(start, qb), h * LANES:(h + 1) * LANES]
        zl.append(lax.dot_general(q[h], kb, (((1,), (1,)), ((), ())), preferred_element_type=F32))
    base, cs = [], []
    for h in heads:
        base_h, cs_h = [], []
        for c in range(nsub):
            z = zl[h][:, c * LANES:(c + 1) * LANES]
            neg_abs = pltpu.bitcast(pltpu.bitcast(z, jnp.uint32) | jnp.uint32(0x80000000), F32)
            s = jnp.maximum(z, 0.0) + jnp.log2(1.0 + jnp.exp2(neg_abs))
            m = valid(c)
            if m is not None:
                s = jnp.where(m, s, 0.0)
            s_hi, s_lo = _split2(s)
            cs_h.append(jnp.dot(s_hi, suffix_ones, preferred_element_type=F32)
                        + jnp.dot(s_lo, suffix_ones, preferred_element_type=F32))
            base_h.append(z - s)
        base.append(base_h)
        cs.append(cs_h)
    w = []
    for h in heads:
        running = later_ref[h]
        parts = [None] * nsub
        for c in reversed(range(nsub)):
            wc = jnp.exp2(base[h][c] - (cs[h][c][:, :LANES] + running))
            m = valid(c)
            if m is not None:
                wc = jnp.where(m, wc, 0.0)
            parts[c] = wc.astype(BF16)
            running = running + cs[h][c][:, LANES:]
        later_ref[h] = running
        w.append(jnp.concatenate(parts, axis=1))
    for h in heads:
        vb = v_ref[0, pl.ds(start, qb), h * LANES:(h + 1) * LANES]
        acc_ref[h] += jnp.dot(w[h], vb, preferred_element_type=F32)


def _sb_kernel(q_ref, k_ref, v_ref, o_ref, acc_ref, later_ref, *, front_pad, n_heads):
    i = pl.program_id(2)
    heads = range(n_heads)
    q = [q_ref[0, :, h * LANES:(h + 1) * LANES] for h in heads]
    rr = lax.broadcasted_iota(jnp.int32, (LANES, 2 * LANES), 0)
    cc = lax.broadcasted_iota(jnp.int32, (LANES, 2 * LANES), 1)
    suffix_ones = jnp.where((cc >= LANES) | (rr > cc), 1.0, 0.0).astype(BF16)
    acc_ref[...] = jnp.zeros_like(acc_ref)
    later_ref[...] = jnp.zeros_like(later_ref)
    block = functools.partial(_sb_block, q=q, k_ref=k_ref, v_ref=v_ref, acc_ref=acc_ref,
                              later_ref=later_ref, suffix_ones=suffix_ones, heads=heads)

    @pl.when(i == 0)
    def _():
        block(0, causal=True, pad=front_pad)

    @pl.when(i > 0)
    def _():
        block(i, causal=True, pad=None)

        def interior(jj, carry):
            block(i - jj, causal=False, pad=None)
            return carry

        lax.fori_loop(1, i, interior, 0)
        block(0, causal=False, pad=front_pad)

    for h in heads:
        o_ref[0, :, h * LANES:(h + 1) * LANES] = acc_ref[h].astype(o_ref.dtype)


def _sb_attention(q, k, v, *, front_pad):
    bsz, p_rows, d = q.shape
    qb = _largest_tile(p_rows, SB_BLOCK_ROWS, LANES)
    assert front_pad <= qb
    n_heads = _largest_tile(d // SB_HEAD, SB_HEADS_PER_STEP, 1)
    width = n_heads * SB_HEAD
    kern = functools.partial(_sb_kernel, front_pad=front_pad, n_heads=n_heads)
    kv_spec = pl.BlockSpec((1, p_rows, width), lambda b, h, i: (b, 0, h))
    q_spec = pl.BlockSpec((1, qb, width), lambda b, h, i: (b, i, h))
    return pl.pallas_call(
        kern,
        out_shape=jax.ShapeDtypeStruct((bsz, p_rows, d), BF16),
        grid=(bsz, d // width, p_rows // qb),
        in_specs=[q_spec, kv_spec, kv_spec],
        out_specs=q_spec,
        scratch_shapes=[pltpu.VMEM((n_heads, qb, SB_HEAD), F32), pltpu.VMEM((n_heads, qb, SB_HEAD), F32)],
        compiler_params=_params("parallel", "parallel", "arbitrary"),
        name="stick_breaking_attention",
    )(q, k, v)


def _rwkv_time_mix(h, v_first, norm_g, mu, w_r, w_k, w_v, w_o, dec_w0, dec_w1, dec_w2, a_w0, a_w1, a_w2,
                   g_w1, g_w2, k_k, k_a, r_k, gn_w, gn_b, vres, *, bsz, p_rows, front_pad):
    t, d = h.shape
    h_prev = jnp.pad(h.reshape(bsz, p_rows, d), ((0, 0), (1, 0), (0, 0)))[:, :-1].reshape(t, d)
    xr, xw, xk, xv, xa, xg = _token_mix(h, h_prev, norm_g, mu, seq_rows=p_rows, front_pad=front_pad)
    v = _matmul(xv, w_v)
    tensors = [_matmul(xr, w_r), _matmul(xk, w_k), v,
               _lora(xw, dec_w1, dec_w2, "tanh"), _lora(xa, a_w1, a_w2, "none"),
               _lora(xg, g_w1, g_w2, "sigmoid")]
    v_w0 = jnp.zeros_like(dec_w0)
    if vres is None:
        v_first = v
    else:
        v_w0, v_w1, v_w2 = vres
        tensors += [_lora(xv, v_w1, v_w2, "none"), v_first]
    vecs = jnp.stack([dec_w0, a_w0, k_k, k_a, r_k.reshape(d), gn_w, gn_b, v_w0])
    gated = _rwkv_core([u.reshape(bsz, p_rows, d) for u in tensors], vecs, vres is not None)
    return _matmul(gated.reshape(t, d), w_o), v_first


def kernel(x, meta_tokens, mix_norm_g, ffn_norm_g, ffn_w_gate, ffn_w_up, ffn_w_down, rwkv_mu, rwkv_w_r, rwkv_w_k, rwkv_w_v, rwkv_w_o, rwkv_dec_w0, rwkv_dec_w1, rwkv_dec_w2, rwkv_a_w0, rwkv_a_w1, rwkv_a_w2, rwkv_g_w1, rwkv_g_w2, rwkv_k_k, rwkv_k_a, rwkv_r_k, rwkv_gn_w, rwkv_gn_b, rwkv_v_w0, rwkv_v_w1, rwkv_v_w2, kv_norm_g, sb_w_k, sb_w_v, sb_k_gain, sb_w_q, sb_q_gain, sb_w_o):
    bsz, seq, d = x.shape
    depth = mix_norm_g.shape[0]
    n_a = rwkv_mu.shape[0]
    front_pad = (-N_META) % BLOCK
    p_rows = front_pad + N_META + seq
    assert p_rows % BLOCK == 0 and p_rows % RWKV_CHUNK == 0
    t = bsz * p_rows

    meta = jnp.broadcast_to(meta_tokens[None].astype(x.dtype), (bsz, N_META, d))
    h = jnp.concatenate([jnp.zeros((bsz, front_pad, d), x.dtype), meta, x], axis=1).reshape(t, d)

    v_first = None
    k_sh = v_sh = None
    for layer in range(depth):
        if layer < n_a:
            i = layer
            vres = None if i == 0 else (rwkv_v_w0[i - 1], rwkv_v_w1[i - 1], rwkv_v_w2[i - 1])
            mix, v_first = _rwkv_time_mix(
                h, v_first, mix_norm_g[layer], rwkv_mu[i], rwkv_w_r[i], rwkv_w_k[i], rwkv_w_v[i],
                rwkv_w_o[i], rwkv_dec_w0[i], rwkv_dec_w1[i], rwkv_dec_w2[i], rwkv_a_w0[i], rwkv_a_w1[i],
                rwkv_a_w2[i], rwkv_g_w1[i], rwkv_g_w2[i], rwkv_k_k[i], rwkv_k_a[i], rwkv_r_k[i],
                rwkv_gn_w[i], rwkv_gn_b[i], vres, bsz=bsz, p_rows=p_rows, front_pad=front_pad)
        else:
            j = layer - n_a
            to3 = lambda u: u.reshape(bsz, p_rows, d)
            if j == 0:
                kvn = _rms_cast(h, kv_norm_g, seq_rows=p_rows)
                k_sh = to3(_matmul(kvn, sb_w_k, BF16, head_gain=sb_k_gain))
                v_sh = to3(_matmul(kvn, sb_w_v, BF16))
            hn = _rms_cast(h, mix_norm_g[layer], seq_rows=p_rows)
            q = _matmul(hn, sb_w_q[j], BF16, head_gain=sb_q_gain[j], head_scale=SB_HEAD ** -0.5 * LOG2E)
            o = _sb_attention(to3(q), k_sh, v_sh, front_pad=front_pad)
            mix = _matmul(o.reshape(t, d), sb_w_o[j])
        h = _ffn(h, mix, ffn_norm_g[layer], ffn_w_gate[layer], ffn_w_up[layer], ffn_w_down[layer],
                 seq_rows=p_rows, front_pad=front_pad)
    return h.reshape(bsz, p_rows, d)[:, front_pad + N_META:]
```

```python
import functools

import jax
import jax.numpy as jnp
from jax import lax
from jax.experimental import pallas as pl
from jax.experimental.pallas import tpu as pltpu

N_META = 16
BLOCK = 128
RWKV_HEAD = 64
SB_HEAD = 128
RMS_EPS = 1e-6
GN_EPS = 64e-5

LANES = 128
RWKV_CHUNK = 64
RWKV_PAIRS_PER_STEP = 16
SB_BLOCK_ROWS = 384
SB_HEADS_PER_STEP = 4
LOG2E = 1.4426950408889634
VMEM_LIMIT_BYTES = 56 * 1024 * 1024

F32 = jnp.float32
BF16 = jnp.bfloat16


def _largest_tile(n, cap, quantum):
    best = None
    for t in range(quantum, min(n, cap) + 1, quantum):
        if n % t == 0:
            best = t
    assert best is not None, (n, cap, quantum)
    return best


def _params(*sem):
    return pltpu.CompilerParams(dimension_semantics=sem, vmem_limit_bytes=VMEM_LIMIT_BYTES)


def _rms_rows(x, g):
    return x * lax.rsqrt(jnp.mean(x * x, axis=-1, keepdims=True) + RMS_EPS) * g


def _softplus(x):
    return jnp.maximum(x, 0.0) + jnp.log(1.0 + jnp.exp(-jnp.abs(x)))


def _mm_kernel(x_ref, w_ref, o_ref):
    o_ref[...] = jnp.dot(x_ref[...], w_ref[...], preferred_element_type=F32).astype(o_ref.dtype)


def _mm_head_rms_kernel(x_ref, w_ref, gain_ref, o_ref, *, scale):
    acc = jnp.dot(x_ref[...], w_ref[...], preferred_element_type=F32)
    gain = gain_ref[...] * scale
    for c in range(acc.shape[1] // SB_HEAD):
        sl = slice(c * SB_HEAD, (c + 1) * SB_HEAD)
        o_ref[:, sl] = _rms_rows(acc[:, sl], gain).astype(o_ref.dtype)


def _matmul(x, w, out_dtype=F32, head_gain=None, head_scale=1.0):
    m, k = x.shape
    n = w.shape[1]
    tm = _largest_tile(m, 1056, 16)
    tn = _largest_tile(n, 1024, LANES)
    in_specs = [pl.BlockSpec((tm, k), lambda j, i: (i, 0)),
                pl.BlockSpec((k, tn), lambda j, i: (0, j))]
    args = [x.astype(BF16), w.astype(BF16)]
    if head_gain is None:
        kern = _mm_kernel
    else:
        kern = functools.partial(_mm_head_rms_kernel, scale=head_scale)
        in_specs.append(pl.BlockSpec((1, SB_HEAD), lambda j, i: (0, 0)))
        args.append(head_gain.reshape(1, SB_HEAD))
    return pl.pallas_call(
        kern,
        out_shape=jax.ShapeDtypeStruct((m, n), out_dtype),
        grid=(n // tn, m // tm),
        in_specs=in_specs,
        out_specs=pl.BlockSpec((tm, tn), lambda j, i: (i, j)),
        compiler_params=_params("parallel", "parallel"),
        name="matmul",
    )(*args)


def _lora_kernel(x_ref, w1_ref, w2_ref, o_ref, *, act):
    mid = jnp.dot(x_ref[...], w1_ref[...], preferred_element_type=F32)
    if act == "tanh":
        mid = jnp.tanh(mid)
    elif act == "sigmoid":
        mid = jax.nn.sigmoid(mid)
    o_ref[...] = jnp.dot(mid.astype(BF16), w2_ref[...], preferred_element_type=F32)


def _lora(x, w1, w2, act):
    m, k = x.shape
    n = w2.shape[1]
    rank = w1.shape[1]
    rp = -(-rank // LANES) * LANES
    w1p = jnp.pad(w1, ((0, 0), (0, rp - rank))).astype(BF16)
    w2p = jnp.pad(w2, ((0, rp - rank), (0, 0))).astype(BF16)
    tm = _largest_tile(m, 1056, 16)
    return pl.pallas_call(
        functools.partial(_lora_kernel, act=act),
        out_shape=jax.ShapeDtypeStruct((m, n), F32),
        grid=(m // tm,),
        in_specs=[pl.BlockSpec((tm, k), lambda i: (i, 0)),
                  pl.BlockSpec((k, rp), lambda i: (0, 0)),
                  pl.BlockSpec((rp, n), lambda i: (0, 0))],
        out_specs=pl.BlockSpec((tm, n), lambda i: (i, 0)),
        compiler_params=_params("parallel"),
        name="lora",
    )(x, w1p, w2p)


def _rms_cast_kernel(h_ref, g_ref, o_ref):
    o_ref[...] = _rms_rows(h_ref[...], g_ref[...]).astype(o_ref.dtype)


def _rms_cast(h, g, *, seq_rows):
    m, d = h.shape
    tm = _largest_tile(seq_rows, 1056, 16)
    spec = pl.BlockSpec((tm, d), lambda i: (i, 0))
    return pl.pallas_call(
        _rms_cast_kernel,
        out_shape=jax.ShapeDtypeStruct((m, d), BF16),
        grid=(m // tm,),
        in_specs=[spec, pl.BlockSpec((1, d), lambda i: (0, 0))],
        out_specs=spec,
        compiler_params=_params("parallel"),
        name="rms_cast",
    )(h, g.reshape(1, d))


def _token_mix_kernel(h_ref, hp_ref, g_ref, mu_ref, *o_refs, tiles_per_seq, front_pad):
    g = g_ref[...]
    hn = _rms_rows(h_ref[...], g)
    xx = _rms_rows(hp_ref[...], g) - hn
    tm = hn.shape[0]
    row = lax.rem(pl.program_id(0), tiles_per_seq) * tm + lax.broadcasted_iota(jnp.int32, (tm, 1), 0)
    keep = row >= front_pad
    for i, o_ref in enumerate(o_refs):
        o_ref[...] = jnp.where(keep, hn + xx * mu_ref[i:i + 1, :], 0.0).astype(o_ref.dtype)


def _token_mix(h, h_prev, g, mu, *, seq_rows, front_pad):
    m, d = h.shape
    n_mix = mu.shape[0]
    tm = _largest_tile(seq_rows, 384, 16)
    spec = pl.BlockSpec((tm, d), lambda i: (i, 0))
    kern = functools.partial(_token_mix_kernel, tiles_per_seq=seq_rows // tm, front_pad=front_pad)
    return pl.pallas_call(
        kern,
        out_shape=[jax.ShapeDtypeStruct((m, d), BF16)] * n_mix,
        grid=(m // tm,),
        in_specs=[spec, spec, pl.BlockSpec((1, d), lambda i: (0, 0)),
                  pl.BlockSpec((n_mix, d), lambda i: (0, 0))],
        out_specs=[spec] * n_mix,
        compiler_params=_params("parallel"),
        name="token_mix",
    )(h, h_prev, g.reshape(1, d), mu)


def _ffn_kernel(h_ref, mix_ref, g_ref, wg_ref, wu_ref, wd_ref, o_ref, xn_ref, *, tiles_per_seq, front_pad):
    f = pl.program_id(1)

    @pl.when(f == 0)
    def _():
        h = h_ref[...] + mix_ref[...]
        xn_ref[...] = _rms_rows(h, g_ref[...]).astype(BF16)
        o_ref[...] = h

    xn = xn_ref[...]
    gate = jnp.dot(xn, wg_ref[...], preferred_element_type=F32)
    up = jnp.dot(xn, wu_ref[...], preferred_element_type=F32)
    act = (gate * jax.nn.sigmoid(gate) * up).astype(BF16)
    o_ref[...] += jnp.dot(act, wd_ref[...], preferred_element_type=F32)

    @pl.when(f == pl.num_programs(1) - 1)
    def _():
        tm = o_ref.shape[0]
        row = lax.rem(pl.program_id(0), tiles_per_seq) * tm + lax.broadcasted_iota(jnp.int32, (tm, 1), 0)
        o_ref[...] = jnp.where(row >= front_pad, o_ref[...], 0.0)


def _ffn(h, mix, g, wg, wu, wd, *, seq_rows, front_pad):
    m, d = h.shape
    dff = wg.shape[1]
    tm = _largest_tile(seq_rows, 528, 16)
    tf = _largest_tile(dff, 512, LANES)
    kern = functools.partial(_ffn_kernel, tiles_per_seq=seq_rows // tm, front_pad=front_pad)
    row_spec = pl.BlockSpec((tm, d), lambda i, f: (i, 0))
    return pl.pallas_call(
        kern,
        out_shape=jax.ShapeDtypeStruct((m, d), F32),
        grid=(m // tm, dff // tf),
        in_specs=[row_spec, row_spec,
                  pl.BlockSpec((1, d), lambda i, f: (0, 0)),
                  pl.BlockSpec((d, tf), lambda i, f: (0, f)),
                  pl.BlockSpec((d, tf), lambda i, f: (0, f)),
                  pl.BlockSpec((tf, d), lambda i, f: (f, 0))],
        out_specs=row_spec,
        scratch_shapes=[pltpu.VMEM((tm, d), BF16)],
        compiler_params=_params("parallel", "arbitrary"),
        name="swiglu_block",
    )(h, mix, g.reshape(1, d), wg.astype(BF16), wu.astype(BF16), wd.astype(BF16))


def _split2(x):
    hi = x.astype(BF16)
    return hi, (x - hi.astype(F32)).astype(BF16)


def _split3(x):
    hi = x.astype(BF16)
    r1 = x - hi.astype(F32)
    mid = r1.astype(BF16)
    lo = (r1 - mid.astype(F32)).astype(BF16)
    return hi, mid, lo


def _rwkv_group_chunk(inputs, vecs, states, consts, use_vres):
    lane_h0, l_incl, m_strict, m_incl, eye, level_masks, head_ones = consts
    pairs = range(len(inputs))
    c = inputs[0][0].shape[0]
    c2 = 2 * c
    nt = lambda x, y: lax.dot_general(x, y, (((1,), (1,)), ((), ())), preferred_element_type=F32)
    nn = lambda x, y: jnp.dot(x, y, preferred_element_type=F32)

    def head_sum(x):
        hi, lo = _split2(x)
        return nn(hi, head_ones) + nn(lo, head_ones)

    def stack(x):
        return jnp.concatenate([jnp.where(lane_h0, x, 0.0), jnp.where(lane_h0, 0.0, x)], axis=0)

    lws, cums, kk_raw, kk_ss = [], [], [], []
    for p in pairs:
        dec = inputs[p][3]
        dec_w0, k_k = vecs[p][0], vecs[p][2]
        wl = -_softplus(-(dec_w0 + dec)) - 0.5
        lw = -jnp.exp(wl)
        hi, mid, lo = _split3(lw)
        cums.append(nn(l_incl, hi) + nn(l_incl, mid) + nn(l_incl, lo))
        lws.append(lw)
        kk0 = inputs[p][1] * k_k
        kk_raw.append(kk0)
        kk_ss.append(head_sum(kk0 * kk0))

    a_s, r_s, v_s, bk_hat, bk_til, decay_all, v_mix, rk = [], [], [], [], [], [], [], []
    for p in pairs:
        r, k, v = inputs[p][0], inputs[p][1], inputs[p][2]
        _, a_w0, _, k_a, r_k = vecs[p][:5]
        al = jax.nn.sigmoid(a_w0 + inputs[p][4])
        kk = kk_raw[p] * lax.rsqrt(jnp.maximum(kk_ss[p], 1e-24))
        k = k * (1.0 + (al - 1.0) * k_a)
        if use_vres:
            v = v + (inputs[p][7] - v) * jax.nn.sigmoid(vecs[p][7] + inputs[p][6])
        v_mix.append(v)
        rk.append(r * k * r_k)
        cum, lw = cums[p], lws[p]
        total = cum[c - 1:c, :]
        e_pos = jnp.exp(cum)
        e_prev = jnp.exp(cum - lw)
        e_neg = jnp.exp(-cum)
        e_rem = jnp.exp(total - cum)
        b = kk * al
        a_s.append(stack(-kk * e_prev).astype(BF16))
        r_s.append(stack(r * e_pos).astype(BF16))
        v_s.append(stack(v).astype(BF16))
        bk_hat.append(jnp.concatenate([stack(b * e_rem), stack(k * e_rem)], axis=0).astype(BF16))
        bt = (b * e_neg).astype(BF16)
        kt = (k * e_neg).astype(BF16)
        bk_til.append(jnp.concatenate([bt, bt, kt, kt], axis=0))
        decay_all.append(jnp.exp(total))

    s_bf = [states[p].astype(BF16) for p in pairs]
    gram = [nt(jnp.concatenate([a_s[p], r_s[p]], axis=0), bk_til[p]) for p in pairs]
    n = [jnp.where(m_strict, gram[p][:c2, :c2], 0.0) for p in pairs]
    n_bf = [x.astype(BF16) for x in n]
    a_ak = [jnp.where(m_strict, gram[p][:c2, c2:], 0.0).astype(BF16) for p in pairs]
    a_rb = [jnp.where(m_incl, gram[p][c2:, :c2], 0.0).astype(BF16) for p in pairs]
    a_rk = [jnp.where(m_incl, gram[p][c2:, c2:], 0.0).astype(BF16) for p in pairs]

    zero_bf = jnp.zeros((), BF16)
    n0 = [jnp.where(level_masks[0], n[p], 0.0) for p in pairs]
    n0_bf = [x.astype(BF16) for x in n0]
    p2 = [nn(n0_bf[p], n0_bf[p]) for p in pairs]
    rhs = [nt(a_s[p], s_bf[p]) + nn(a_ak[p], v_s[p]) for p in pairs]
    p2_bf = [x.astype(BF16) for x in p2]
    p4 = [nn(p2_bf[p], p2_bf[p]) for p in pairs]
    x = [eye + n0[p] + p2[p] + nn(n0_bf[p], p2_bf[p]) for p in pairs]
    y_part = [nt(r_s[p], s_bf[p]) + nn(a_rk[p], v_s[p]) for p in pairs]
    bonus_dot = [head_sum(rk[p]) for p in pairs]
    x = [x[p] + nn(x[p].astype(BF16), p4[p].astype(BF16)) for p in pairs]
    for off_mask in level_masks[1:]:
        x_bf = [x[p].astype(BF16) for p in pairs]
        t = [nn(x_bf[p], jnp.where(off_mask, n_bf[p], zero_bf)) for p in pairs]
        x = [x[p] + nn(t[p].astype(BF16), x_bf[p]) for p in pairs]

    sa = [nn(x[p].astype(BF16), rhs[p].astype(BF16)).astype(BF16) for p in pairs]
    ys, s_new = [], []
    for p in pairs:
        y_s = y_part[p] + nn(a_rb[p], sa[p])
        ys.append(y_s[:c, :] + y_s[c:, :])
    for p in pairs:
        lhs_t = jnp.concatenate([sa[p], v_s[p]], axis=0).astype(F32).T.astype(BF16)
        s_new.append(states[p] * decay_all[p] + nn(lhs_t, bk_hat[p]))

    inv_n = 1.0 / RWKV_HEAD
    means = [head_sum(ys[p]) * inv_n for p in pairs]
    cent = [ys[p] - means[p] for p in pairs]
    var = [head_sum(cent[p] * cent[p]) * inv_n for p in pairs]
    outs = []
    for p in pairs:
        gn_w, gn_b = vecs[p][5], vecs[p][6]
        yn = cent[p] * lax.rsqrt(var[p] + GN_EPS) * gn_w + gn_b
        outs.append(((yn + bonus_dot[p] * v_mix[p]) * inputs[p][5]).astype(BF16))
    return outs, s_new


def _rwkv_consts(c):
    c2 = 2 * c
    row = lax.broadcasted_iota(jnp.int32, (c2, c2), 0)
    col = lax.broadcasted_iota(jnp.int32, (c2, c2), 1)

    def same(size):
        sh = size.bit_length() - 1
        return lax.shift_right_logical(row, sh) == lax.shift_right_logical(col, sh)

    same_head = same(c)
    m_strict = same_head & (col < row)
    m_incl = same_head & (col <= row)
    eye = jnp.where(row == col, 1.0, 0.0).astype(F32)
    levels = [same(8)]
    size = 16
    while size <= c:
        levels.append(same(size) & jnp.logical_not(same(size // 2)))
        size *= 2
    lane_h0 = lax.broadcasted_iota(jnp.int32, (c, LANES), 1) < RWKV_HEAD
    rr = lax.broadcasted_iota(jnp.int32, (c, c), 0)
    cc = lax.broadcasted_iota(jnp.int32, (c, c), 1)
    l_incl = jnp.where(cc <= rr, 1.0, 0.0).astype(BF16)
    head_ones = jnp.where(same(RWKV_HEAD), 1.0, 0.0).astype(BF16)
    return lane_h0, l_incl, m_strict, m_incl, eye, levels, head_ones


def _rwkv_kernel(*refs, pairs, n_in, use_vres):
    in_refs, vec_ref, o_ref, s_ref = refs[:n_in], refs[n_in], refs[n_in + 1], refs[n_in + 2]

    @pl.when(pl.program_id(2) == 0)
    def _():
        s_ref[...] = jnp.zeros_like(s_ref)

    consts = _rwkv_consts(in_refs[0].shape[1])
    lanes = [slice(p * LANES, (p + 1) * LANES) for p in range(pairs)]
    inputs = [tuple(ref[0, :, sl] for ref in in_refs) for sl in lanes]
    vecs = [tuple(vec_ref[i:i + 1, sl] for i in range(vec_ref.shape[0])) for sl in lanes]
    outs, s_new = _rwkv_group_chunk(inputs, vecs, [s_ref[p] for p in range(pairs)], consts, use_vres)
    for p in range(pairs):
        o_ref[0, :, lanes[p]] = outs[p]
        s_ref[p] = s_new[p]


def _rwkv_core(tensors, vecs, use_vres):
    bsz, p_rows, d = tensors[0].shape
    c = RWKV_CHUNK
    pairs = _largest_tile(d // LANES, RWKV_PAIRS_PER_STEP, 1)
    width = pairs * LANES
    spec = pl.BlockSpec((1, c, width), lambda b, g, t: (b, t, g))
    n_in = len(tensors)
    return pl.pallas_call(
        functools.partial(_rwkv_kernel, pairs=pairs, n_in=n_in, use_vres=use_vres),
        out_shape=jax.ShapeDtypeStruct((bsz, p_rows, d), BF16),
        grid=(bsz, d // width, p_rows // c),
        in_specs=[spec] * n_in + [pl.BlockSpec((vecs.shape[0], width), lambda b, g, t: (0, g))],
        out_specs=spec,
        scratch_shapes=[pltpu.VMEM((pairs, LANES, LANES), F32)],
        compiler_params=_params("parallel", "parallel", "arbitrary"),
        name="rwkv7_scan",
    )(*tensors, vecs)


def _sb_block(j, q, k_ref, v_ref, acc_ref, later_ref, suffix_ones, heads, *, causal, pad):
    qb = q[0].shape[0]
    nsub = qb // LANES
    start = pl.multiple_of(j * qb, qb)
    row = lax.broadcasted_iota(jnp.int32, (qb, LANES), 0)
    lane = lax.broadcasted_iota(jnp.int32, (qb, LANES), 1)

    def valid(c):
        col = lane + c * LANES
        use_pad = pad is not None and c * LANES < pad
        if causal and use_pad:
            return (col < row) & (col >= pad)
        if causal:
            return col < row
        if use_pad:
            return col >= pad
        return None

    zl = []
    for h in heads:
        kb = k_ref[0, pl.ds(start, qb), h * LANES:(h + 1) * LANES]
        zl.append(lax.dot_general(q[h], kb, (((1,), (1,)), ((), ())), preferred_element_type=F32))
    base, cs = [], []
    for h in heads:
        base_h, cs_h = [], []
        for c in range(nsub):
            z = zl[h][:, c * LANES:(c + 1) * LANES]
            neg_abs = pltpu.bitcast(pltpu.bitcast(z, jnp.uint32) | jnp.uint32(0x80000000), F32)
            s = jnp.maximum(z, 0.0) + jnp.log2(1.0 + jnp.exp2(neg_abs))
            m = valid(c)
            if m is not None:
                s = jnp.where(m, s, 0.0)
            cs_h.append(jnp.dot(s.astype(BF16), suffix_ones, preferred_element_type=F32))
            base_h.append(z - s)
        base.append(base_h)
        cs.append(cs_h)
    w = []
    for h in heads:
        running = later_ref[h]
        parts = [None] * nsub
        for c in reversed(range(nsub)):
            wc = jnp.exp2(base[h][c] - (cs[h][c][:, :LANES] + running))
            m = valid(c)
            if m is not None:
                wc = jnp.where(m, wc, 0.0)
            parts[c] = wc.astype(BF16)
            running = running + cs[h][c][:, LANES:]
        later_ref[h] = running
        w.append(jnp.concatenate(parts, axis=1))
    for h in heads:
        vb = v_ref[0, pl.ds(start, qb), h * LANES:(h + 1) * LANES]
        acc_ref[h] += jnp.dot(w[h], vb, preferred_element_type=F32)


def _sb_kernel(q_ref, k_ref, v_ref, o_ref, acc_ref, later_ref, *, front_pad, n_heads):
    i = pl.program_id(2)
    heads = range(n_heads)
    q = [q_ref[0, :, h * LANES:(h + 1) * LANES] for h in heads]
    rr = lax.broadcasted_iota(jnp.int32, (LANES, 2 * LANES), 0)
    cc = lax.broadcasted_iota(jnp.int32, (LANES, 2 * LANES), 1)
    suffix_ones = jnp.where((cc >= LANES) | (rr > cc), 1.0, 0.0).astype(BF16)
    acc_ref[...] = jnp.zeros_like(acc_ref)
    later_ref[...] = jnp.zeros_like(later_ref)
    block = functools.partial(_sb_block, q=q, k_ref=k_ref, v_ref=v_ref, acc_ref=acc_ref,
                              later_ref=later_ref, suffix_ones=suffix_ones, heads=heads)

    @pl.when(i == 0)
    def _():
        block(0, causal=True, pad=front_pad)

    @pl.when(i > 0)
    def _():
        block(i, causal=True, pad=None)

        def interior(jj, carry):
            block(i - jj, causal=False, pad=None)
            return carry

        lax.fori_loop(1, i, interior, 0)
        block(0, causal=False, pad=front_pad)

    for h in heads:
        o_ref[0, :, h * LANES:(h + 1) * LANES] = acc_ref[h].astype(o_ref.dtype)


def _sb_attention(q, k, v, *, front_pad):
    bsz, p_rows, d = q.shape
    qb = _largest_tile(p_rows, SB_BLOCK_ROWS, LANES)
    assert front_pad <= qb
    n_heads = _largest_tile(d // SB_HEAD, SB_HEADS_PER_STEP, 1)
    width = n_heads * SB_HEAD
    kern = functools.partial(_sb_kernel, front_pad=front_pad, n_heads=n_heads)
    kv_spec = pl.BlockSpec((1, p_rows, width), lambda b, h, i: (b, 0, h))
    q_spec = pl.BlockSpec((1, qb, width), lambda b, h, i: (b, i, h))
    return pl.pallas_call(
        kern,
        out_shape=jax.ShapeDtypeStruct((bsz, p_rows, d), BF16),
        grid=(bsz, d // width, p_rows // qb),
        in_specs=[q_spec, kv_spec, kv_spec],
        out_specs=q_spec,
        scratch_shapes=[pltpu.VMEM((n_heads, qb, SB_HEAD), F32), pltpu.VMEM((n_heads, qb, SB_HEAD), F32)],
        compiler_params=_params("parallel", "parallel", "arbitrary"),
        name="stick_breaking_attention",
    )(q, k, v)


def _rwkv_time_mix(h, v_first, norm_g, mu, w_r, w_k, w_v, w_o, dec_w0, dec_w1, dec_w2, a_w0, a_w1, a_w2,
                   g_w1, g_w2, k_k, k_a, r_k, gn_w, gn_b, vres, *, bsz, p_rows, front_pad):
    t, d = h.shape
    h_prev = jnp.pad(h.reshape(bsz, p_rows, d), ((0, 0), (1, 0), (0, 0)))[:, :-1].reshape(t, d)
    xr, xw, xk, xv, xa, xg = _token_mix(h, h_prev, norm_g, mu, seq_rows=p_rows, front_pad=front_pad)
    v = _matmul(xv, w_v)
    tensors = [_matmul(xr, w_r), _matmul(xk, w_k), v,
               _lora(xw, dec_w1, dec_w2, "tanh"), _lora(xa, a_w1, a_w2, "none"),
               _lora(xg, g_w1, g_w2, "sigmoid")]
    v_w0 = jnp.zeros_like(dec_w0)
    if vres is None:
        v_first = v
    else:
        v_w0, v_w1, v_w2 = vres
        tensors += [_lora(xv, v_w1, v_w2, "none"), v_first]
    vecs = jnp.stack([dec_w0, a_w0, k_k, k_a, r_k.reshape(d), gn_w, gn_b, v_w0])
    gated = _rwkv_core([u.reshape(bsz, p_rows, d) for u in tensors], vecs, vres is not None)
    return _matmul(gated.reshape(t, d), w_o), v_first


def kernel(x, meta_tokens, mix_norm_g, ffn_norm_g, ffn_w_gate, ffn_w_up, ffn_w_down, rwkv_mu, rwkv_w_r, rwkv_w_k, rwkv_w_v, rwkv_w_o, rwkv_dec_w0, rwkv_dec_w1, rwkv_dec_w2, rwkv_a_w0, rwkv_a_w1, rwkv_a_w2, rwkv_g_w1, rwkv_g_w2, rwkv_k_k, rwkv_k_a, rwkv_r_k, rwkv_gn_w, rwkv_gn_b, rwkv_v_w0, rwkv_v_w1, rwkv_v_w2, kv_norm_g, sb_w_k, sb_w_v, sb_k_gain, sb_w_q, sb_q_gain, sb_w_o):
    bsz, seq, d = x.shape
    depth = mix_norm_g.shape[0]
    n_a = rwkv_mu.shape[0]
    front_pad = (-N_META) % BLOCK
    p_rows = front_pad + N_META + seq
    assert p_rows % BLOCK == 0 and p_rows % RWKV_CHUNK == 0
    t = bsz * p_rows

    meta = jnp.broadcast_to(meta_tokens[None].astype(x.dtype), (bsz, N_META, d))
    h = jnp.concatenate([jnp.zeros((bsz, front_pad, d), x.dtype), meta, x], axis=1).reshape(t, d)

    v_first = None
    k_sh = v_sh = None
    for layer in range(depth):
        if layer < n_a:
            i = layer
            vres = None if i == 0 else (rwkv_v_w0[i - 1], rwkv_v_w1[i - 1], rwkv_v_w2[i - 1])
            mix, v_first = _rwkv_time_mix(
                h, v_first, mix_norm_g[layer], rwkv_mu[i], rwkv_w_r[i], rwkv_w_k[i], rwkv_w_v[i],
                rwkv_w_o[i], rwkv_dec_w0[i], rwkv_dec_w1[i], rwkv_dec_w2[i], rwkv_a_w0[i], rwkv_a_w1[i],
                rwkv_a_w2[i], rwkv_g_w1[i], rwkv_g_w2[i], rwkv_k_k[i], rwkv_k_a[i], rwkv_r_k[i],
                rwkv_gn_w[i], rwkv_gn_b[i], vres, bsz=bsz, p_rows=p_rows, front_pad=front_pad)
        else:
            j = layer - n_a
            to3 = lambda u: u.reshape(bsz, p_rows, d)
            if j == 0:
                kvn = _rms_cast(h, kv_norm_g, seq_rows=p_rows)
                k_sh = to3(_matmul(kvn, sb_w_k, BF16, head_gain=sb_k_gain))
                v_sh = to3(_matmul(kvn, sb_w_v, BF16))
            hn = _rms_cast(h, mix_norm_g[layer], seq_rows=p_rows)
            q = _matmul(hn, sb_w_q[j], BF16, head_gain=sb_q_gain[j], head_scale=SB_HEAD ** -0.5 * LOG2E)
            o = _sb_attention(to3(q), k_sh, v_sh, front_pad=front_pad)
            mix = _matmul(o.reshape(t, d), sb_w_o[j])
        h = _ffn(h, mix, ffn_norm_g[layer], ffn_w_gate[layer], ffn_w_up[layer], ffn_w_down[layer],
                 seq_rows=p_rows, front_pad=front_pad)
    return h.reshape(bsz, p_rows, d)[:, front_pad + N_META:]
```

```python
import functools

import jax
import jax.numpy as jnp
from jax import lax
from jax.experimental import pallas as pl
from jax.experimental.pallas import tpu as pltpu

N_META = 16
BLOCK = 128
RWKV_HEAD = 64
SB_HEAD = 128
RMS_EPS = 1e-6
GN_EPS = 64e-5

LANES = 128
SUBLANES = 8
TOKEN_MIX_GROUP = 16
RWKV_CHUNK = 64
RWKV_PAIRS_PER_STEP = 16
SB_BLOCK_ROWS = 384
SB_HEADS_PER_STEP = 4
SB_PIPE_SKEW = 2
LOG2E = 1.4426950408889634
VMEM_LIMIT_BYTES = 56 * 1024 * 1024

F32 = jnp.float32
BF16 = jnp.bfloat16


def _largest_tile(n, cap, quantum):
    best = None
    for t in range(quantum, min(n, cap) + 1, quantum):
        if n % t == 0:
            best = t
    assert best is not None, (n, cap, quantum)
    return best


def _params(*sem):
    return pltpu.CompilerParams(dimension_semantics=sem, vmem_limit_bytes=VMEM_LIMIT_BYTES)


def _rms_rows(x, g):
    return x * lax.rsqrt(jnp.mean(x * x, axis=-1, keepdims=True) + RMS_EPS) * g


def _softplus(x):
    return jnp.maximum(x, 0.0) + jnp.log(1.0 + jnp.exp(-jnp.abs(x)))


def _mm_kernel(x_ref, w_ref, o_ref):
    o_ref[...] = jnp.dot(x_ref[...], w_ref[...], preferred_element_type=F32).astype(o_ref.dtype)


def _mm_head_rms_kernel(x_ref, w_ref, gain_ref, o_ref, *, scale):
    acc = jnp.dot(x_ref[...], w_ref[...], preferred_element_type=F32)
    gain = gain_ref[...] * scale
    for c in range(acc.shape[1] // SB_HEAD):
        sl = slice(c * SB_HEAD, (c + 1) * SB_HEAD)
        o_ref[:, sl] = _rms_rows(acc[:, sl], gain).astype(o_ref.dtype)


def _matmul(x, w, out_dtype=F32, head_gain=None, head_scale=1.0):
    m, k = x.shape
    n = w.shape[1]
    tm = _largest_tile(m, 1056, 16)
    tn = _largest_tile(n, 1024, LANES)
    in_specs = [pl.BlockSpec((tm, k), lambda j, i: (i, 0)),
                pl.BlockSpec((k, tn), lambda j, i: (0, j))]
    args = [x.astype(BF16), w.astype(BF16)]
    if head_gain is None:
        kern = _mm_kernel
    else:
        kern = functools.partial(_mm_head_rms_kernel, scale=head_scale)
        in_specs.append(pl.BlockSpec((1, SB_HEAD), lambda j, i: (0, 0)))
        args.append(head_gain.reshape(1, SB_HEAD))
    return pl.pallas_call(
        kern,
        out_shape=jax.ShapeDtypeStruct((m, n), out_dtype),
        grid=(n // tn, m // tm),
        in_specs=in_specs,
        out_specs=pl.BlockSpec((tm, tn), lambda j, i: (i, j)),
        compiler_params=_params("parallel", "parallel"),
        name="matmul",
    )(*args)


def _lora_kernel(x_ref, w1_ref, w2_ref, o_ref, *, act):
    mid = jnp.dot(x_ref[...], w1_ref[...], preferred_element_type=F32)
    if act == "tanh":
        mid = jnp.tanh(mid)
    elif act == "sigmoid":
        mid = jax.nn.sigmoid(mid)
    o_ref[...] = jnp.dot(mid.astype(BF16), w2_ref[...], preferred_element_type=F32)


def _lora(x, w1, w2, act):
    m, k = x.shape
    n = w2.shape[1]
    rank = w1.shape[1]
    rp = -(-rank // LANES) * LANES
    w1p = jnp.pad(w1, ((0, 0), (0, rp - rank))).astype(BF16)
    w2p = jnp.pad(w2, ((0, rp - rank), (0, 0))).astype(BF16)
    tm = _largest_tile(m, 1056, 16)
    return pl.pallas_call(
        functools.partial(_lora_kernel, act=act),
        out_shape=jax.ShapeDtypeStruct((m, n), F32),
        grid=(m // tm,),
        in_specs=[pl.BlockSpec((tm, k), lambda i: (i, 0)),
                  pl.BlockSpec((k, rp), lambda i: (0, 0)),
                  pl.BlockSpec((rp, n), lambda i: (0, 0))],
        out_specs=pl.BlockSpec((tm, n), lambda i: (i, 0)),
        compiler_params=_params("parallel"),
        name="lora",
    )(x, w1p, w2p)


def _rms_cast_kernel(h_ref, g_ref, o_ref):
    o_ref[...] = _rms_rows(h_ref[...], g_ref[...]).astype(o_ref.dtype)


def _rms_cast(h, g, *, seq_rows):
    m, d = h.shape
    tm = _largest_tile(seq_rows, 1056, 16)
    spec = pl.BlockSpec((tm, d), lambda i: (i, 0))
    return pl.pallas_call(
        _rms_cast_kernel,
        out_shape=jax.ShapeDtypeStruct((m, d), BF16),
        grid=(m // tm,),
        in_specs=[spec, pl.BlockSpec((1, d), lambda i: (0, 0))],
        out_specs=spec,
        compiler_params=_params("parallel"),
        name="rms_cast",
    )(h, g.reshape(1, d))


def _token_mix_kernel(h_ref, tail_ref, g_ref, mu_ref, *o_refs, tiles_per_seq, front_pad):
    g = g_ref[...]
    tm = h_ref.shape[0]
    grp = TOKEN_MIX_GROUP
    row0 = lax.rem(pl.program_id(0), tiles_per_seq) * tm
    sub = lax.broadcasted_iota(jnp.int32, (grp, 1), 0)

    def body(r, prev_tail):
        start = pl.multiple_of(r * grp, grp)
        hn = _rms_rows(h_ref[pl.ds(start, grp), :], g)
        shifted = jnp.where(sub == 0, prev_tail[SUBLANES - 1:SUBLANES, :], pltpu.roll(hn, 1, 0))
        xx = shifted - hn
        keep = row0 + start + sub >= front_pad
        for i, o_ref in enumerate(o_refs):
            o_ref[pl.ds(start, grp), :] = jnp.where(keep, hn + xx * mu_ref[i:i + 1, :], 0.0).astype(o_ref.dtype)
        return hn[grp - SUBLANES:, :]

    lax.fori_loop(0, tm // grp, body, _rms_rows(tail_ref[...], g))


def _token_mix(h, g, mu, *, seq_rows, front_pad):
    m, d = h.shape
    n_mix = mu.shape[0]
    tm = _largest_tile(seq_rows, 528, TOKEN_MIX_GROUP)
    spec = pl.BlockSpec((tm, d), lambda i: (i, 0))
    tail_spec = pl.BlockSpec((SUBLANES, d), lambda i: (jnp.maximum(i * (tm // SUBLANES) - 1, 0), 0))
    kern = functools.partial(_token_mix_kernel, tiles_per_seq=seq_rows // tm, front_pad=front_pad)
    return pl.pallas_call(
        kern,
        out_shape=[jax.ShapeDtypeStruct((m, d), BF16)] * n_mix,
        grid=(m // tm,),
        in_specs=[spec, tail_spec, pl.BlockSpec((1, d), lambda i: (0, 0)),
                  pl.BlockSpec((n_mix, d), lambda i: (0, 0))],
        out_specs=[spec] * n_mix,
        compiler_params=_params("parallel"),
        name="token_mix",
    )(h, h, g.reshape(1, d), mu)


def _ffn_kernel(h_ref, mix_ref, g_ref, wg_ref, wu_ref, wd_ref, o_ref, xn_ref, *, tiles_per_seq, front_pad):
    f = pl.program_id(1)

    @pl.when(f == 0)
    def _():
        h = h_ref[...] + mix_ref[...]
        xn_ref[...] = _rms_rows(h, g_ref[...]).astype(BF16)
        o_ref[...] = h

    xn = xn_ref[...]
    gate = jnp.dot(xn, wg_ref[...], preferred_element_type=F32)
    up = jnp.dot(xn, wu_ref[...], preferred_element_type=F32)
    act = (gate * jax.nn.sigmoid(gate) * up).astype(BF16)
    o_ref[...] += jnp.dot(act, wd_ref[...], preferred_element_type=F32)

    @pl.when(f == pl.num_programs(1) - 1)
    def _():
        tm = o_ref.shape[0]
        row = lax.rem(pl.program_id(0), tiles_per_seq) * tm + lax.broadcasted_iota(jnp.int32, (tm, 1), 0)
        o_ref[...] = jnp.where(row >= front_pad, o_ref[...], 0.0)


def _ffn(h, mix, g, wg, wu, wd, *, seq_rows, front_pad):
    m, d = h.shape
    dff = wg.shape[1]
    tm = _largest_tile(seq_rows, 528, 16)
    tf = _largest_tile(dff, 512, LANES)
    kern = functools.partial(_ffn_kernel, tiles_per_seq=seq_rows // tm, front_pad=front_pad)
    row_spec = pl.BlockSpec((tm, d), lambda i, f: (i, 0))
    return pl.pallas_call(
        kern,
        out_shape=jax.ShapeDtypeStruct((m, d), F32),
        grid=(m // tm, dff // tf),
        in_specs=[row_spec, row_spec,
                  pl.BlockSpec((1, d), lambda i, f: (0, 0)),
                  pl.BlockSpec((d, tf), lambda i, f: (0, f)),
                  pl.BlockSpec((d, tf), lambda i, f: (0, f)),
                  pl.BlockSpec((tf, d), lambda i, f: (f, 0))],
        out_specs=row_spec,
        scratch_shapes=[pltpu.VMEM((tm, d), BF16)],
        compiler_params=_params("parallel", "arbitrary"),
        name="swiglu_block",
    )(h, mix, g.reshape(1, d), wg.astype(BF16), wu.astype(BF16), wd.astype(BF16))


def _split2(x):
    hi = x.astype(BF16)
    return hi, (x - hi.astype(F32)).astype(BF16)


def _split3(x):
    hi = x.astype(BF16)
    r1 = x - hi.astype(F32)
    mid = r1.astype(BF16)
    lo = (r1 - mid.astype(F32)).astype(BF16)
    return hi, mid, lo


def _rwkv_group_chunk(inputs, vecs, states, consts, use_vres):
    lane_h0, l_incl, m_strict, m_incl, eye, level_masks, head_ones = consts
    pairs = range(len(inputs))
    c = inputs[0][0].shape[0]
    c2 = 2 * c
    nt = lambda x, y: lax.dot_general(x, y, (((1,), (1,)), ((), ())), preferred_element_type=F32)
    nn = lambda x, y: jnp.dot(x, y, preferred_element_type=F32)

    def head_sum(x):
        hi, lo = _split2(x)
        return nn(hi, head_ones) + nn(lo, head_ones)

    def stack(x):
        return jnp.concatenate([jnp.where(lane_h0, x, 0.0), jnp.where(lane_h0, 0.0, x)], axis=0)

    lws, cums, kk_raw, kk_ss = [], [], [], []
    for p in pairs:
        dec = inputs[p][3]
        dec_w0, k_k = vecs[p][0], vecs[p][2]
        wl = -_softplus(-(dec_w0 + dec)) - 0.5
        lw = -jnp.exp(wl)
        hi, mid, lo = _split3(lw)
        cums.append(nn(l_incl, hi) + nn(l_incl, mid) + nn(l_incl, lo))
        lws.append(lw)
        kk0 = inputs[p][1] * k_k
        kk_raw.append(kk0)
        kk_ss.append(head_sum(kk0 * kk0))

    a_s, r_s, v_s, bk_hat, bk_til, decay_all, v_mix, rk = [], [], [], [], [], [], [], []
    for p in pairs:
        r, k, v = inputs[p][0], inputs[p][1], inputs[p][2]
        _, a_w0, _, k_a, r_k = vecs[p][:5]
        al = jax.nn.sigmoid(a_w0 + inputs[p][4])
        kk = kk_raw[p] * lax.rsqrt(jnp.maximum(kk_ss[p], 1e-24))
        k = k * (1.0 + (al - 1.0) * k_a)
        if use_vres:
            v = v + (inputs[p][7] - v) * jax.nn.sigmoid(vecs[p][7] + inputs[p][6])
        v_mix.append(v)
        rk.append(r * k * r_k)
        cum, lw = cums[p], lws[p]
        total = cum[c - 1:c, :]
        e_pos = jnp.exp(cum)
        e_prev = jnp.exp(cum - lw)
        e_neg = jnp.exp(-cum)
        e_rem = jnp.exp(total - cum)
        b = kk * al
        a_s.append(stack(-kk * e_prev).astype(BF16))
        r_s.append(stack(r * e_pos).astype(BF16))
        v_s.append(stack(v).astype(BF16))
        bk_hat.append(jnp.concatenate([stack(b * e_rem), stack(k * e_rem)], axis=0).astype(BF16))
        bt = (b * e_neg).astype(BF16)
        kt = (k * e_neg).astype(BF16)
        bk_til.append(jnp.concatenate([bt, bt, kt, kt], axis=0))
        decay_all.append(jnp.exp(total))

    s_bf = [states[p].astype(BF16) for p in pairs]
    gram = [nt(jnp.concatenate([a_s[p], r_s[p]], axis=0), bk_til[p]) for p in pairs]
    n = [jnp.where(m_strict, gram[p][:c2, :c2], 0.0) for p in pairs]
    n_bf = [x.astype(BF16) for x in n]
    a_ak = [jnp.where(m_strict, gram[p][:c2, c2:], 0.0).astype(BF16) for p in pairs]
    a_rb = [jnp.where(m_incl, gram[p][c2:, :c2], 0.0).astype(BF16) for p in pairs]
    a_rk = [jnp.where(m_incl, gram[p][c2:, c2:], 0.0).astype(BF16) for p in pairs]

    zero_bf = jnp.zeros((), BF16)
    n0 = [jnp.where(level_masks[0], n[p], 0.0) for p in pairs]
    n0_bf = [x.astype(BF16) for x in n0]
    p2 = [nn(n0_bf[p], n0_bf[p]) for p in pairs]
    rhs = [nt(a_s[p], s_bf[p]) + nn(a_ak[p], v_s[p]) for p in pairs]
    p2_bf = [x.astype(BF16) for x in p2]
    p4 = [nn(p2_bf[p], p2_bf[p]) for p in pairs]
    x = [eye + n0[p] + p2[p] + nn(n0_bf[p], p2_bf[p]) for p in pairs]
    y_part = [nt(r_s[p], s_bf[p]) + nn(a_rk[p], v_s[p]) for p in pairs]
    bonus_dot = [head_sum(rk[p]) for p in pairs]
    x = [x[p] + nn(x[p].astype(BF16), p4[p].astype(BF16)) for p in pairs]
    for off_mask in level_masks[1:]:
        x_bf = [x[p].astype(BF16) for p in pairs]
        t = [nn(x_bf[p], jnp.where(off_mask, n_bf[p], zero_bf)) for p in pairs]
        x = [x[p] + nn(t[p].astype(BF16), x_bf[p]) for p in pairs]

    sa = [nn(x[p].astype(BF16), rhs[p].astype(BF16)).astype(BF16) for p in pairs]
    ys, s_new = [], []
    for p in pairs:
        y_s = y_part[p] + nn(a_rb[p], sa[p])
        ys.append(y_s[:c, :] + y_s[c:, :])
    for p in pairs:
        lhs_t = jnp.concatenate([sa[p], v_s[p]], axis=0).astype(F32).T.astype(BF16)
        s_new.append(states[p] * decay_all[p] + nn(lhs_t, bk_hat[p]))

    inv_n = 1.0 / RWKV_HEAD
    means = [head_sum(ys[p]) * inv_n for p in pairs]
    cent = [ys[p] - means[p] for p in pairs]
    var = [head_sum(cent[p] * cent[p]) * inv_n for p in pairs]
    outs = []
    for p in pairs:
        gn_w, gn_b = vecs[p][5], vecs[p][6]
        yn = cent[p] * lax.rsqrt(var[p] + GN_EPS) * gn_w + gn_b
        outs.append(((yn + bonus_dot[p] * v_mix[p]) * inputs[p][5]).astype(BF16))
    return outs, s_new


def _rwkv_consts(c):
    c2 = 2 * c
    row = lax.broadcasted_iota(jnp.int32, (c2, c2), 0)
    col = lax.broadcasted_iota(jnp.int32, (c2, c2), 1)

    def same(size):
        sh = size.bit_length() - 1
        return lax.shift_right_logical(row, sh) == lax.shift_right_logical(col, sh)

    same_head = same(c)
    m_strict = same_head & (col < row)
    m_incl = same_head & (col <= row)
    eye = jnp.where(row == col, 1.0, 0.0).astype(F32)
    levels = [same(8)]
    size = 16
    while size <= c:
        levels.append(same(size) & jnp.logical_not(same(size // 2)))
        size *= 2
    lane_h0 = lax.broadcasted_iota(jnp.int32, (c, LANES), 1) < RWKV_HEAD
    rr = lax.broadcasted_iota(jnp.int32, (c, c), 0)
    cc = lax.broadcasted_iota(jnp.int32, (c, c), 1)
    l_incl = jnp.where(cc <= rr, 1.0, 0.0).astype(BF16)
    head_ones = jnp.where(same(RWKV_HEAD), 1.0, 0.0).astype(BF16)
    return lane_h0, l_incl, m_strict, m_incl, eye, levels, head_ones


def _rwkv_kernel(*refs, pairs, n_in, use_vres):
    in_refs, vec_ref, o_ref, s_ref = refs[:n_in], refs[n_in], refs[n_in + 1], refs[n_in + 2]

    @pl.when(pl.program_id(2) == 0)
    def _():
        s_ref[...] = jnp.zeros_like(s_ref)

    consts = _rwkv_consts(in_refs[0].shape[1])
    lanes = [slice(p * LANES, (p + 1) * LANES) for p in range(pairs)]
    inputs = [tuple(ref[0, :, sl] for ref in in_refs) for sl in lanes]
    vecs = [tuple(vec_ref[i:i + 1, sl] for i in range(vec_ref.shape[0])) for sl in lanes]
    outs, s_new = _rwkv_group_chunk(inputs, vecs, [s_ref[p] for p in range(pairs)], consts, use_vres)
    for p in range(pairs):
        o_ref[0, :, lanes[p]] = outs[p]
        s_ref[p] = s_new[p]


def _rwkv_core(tensors, vecs, use_vres):
    bsz, p_rows, d = tensors[0].shape
    c = RWKV_CHUNK
    pairs = _largest_tile(d // LANES, RWKV_PAIRS_PER_STEP, 1)
    width = pairs * LANES
    spec = pl.BlockSpec((1, c, width), lambda b, g, t: (b, t, g))
    n_in = len(tensors)
    return pl.pallas_call(
        functools.partial(_rwkv_kernel, pairs=pairs, n_in=n_in, use_vres=use_vres),
        out_shape=jax.ShapeDtypeStruct((bsz, p_rows, d), BF16),
        grid=(bsz, d // width, p_rows // c),
        in_specs=[spec] * n_in + [pl.BlockSpec((vecs.shape[0], width), lambda b, g, t: (0, g))],
        out_specs=spec,
        scratch_shapes=[pltpu.VMEM((pairs, LANES, LANES), F32)],
        compiler_params=_params("parallel", "parallel", "arbitrary"),
        name="rwkv7_scan",
    )(*tensors, vecs)


def _sb_block(j, q, k_ref, v_ref, acc_ref, later_ref, suffix_ones, heads, *, causal, pad):
    qb = q[0].shape[0]
    nsub = qb // LANES
    start = pl.multiple_of(j * qb, qb)

    def first_row(c):
        return c * LANES if causal else 0

    def valid(c):
        rows = qb - first_row(c)
        row = lax.broadcasted_iota(jnp.int32, (rows, LANES), 0) + first_row(c)
        col = lax.broadcasted_iota(jnp.int32, (rows, LANES), 1) + c * LANES
        use_pad = pad is not None and c * LANES < pad
        if causal and use_pad:
            return (col < row) & (col >= pad)
        if causal:
            return col < row
        if use_pad:
            return col >= pad
        return None

    units = [(h, c) for h in heads for c in reversed(range(nsub))]

    def scores(u):
        h, c = u
        kb = k_ref[0, pl.ds(pl.multiple_of(start + c * LANES, LANES), LANES), h * LANES:(h + 1) * LANES]
        return lax.dot_general(q[h][first_row(c):, :], kb, (((1,), (1,)), ((), ())),
                               preferred_element_type=F32)

    def log_keep(u, z):
        neg_abs = pltpu.bitcast(pltpu.bitcast(z, jnp.uint32) | jnp.uint32(0x80000000), F32)
        s = jnp.maximum(z, 0.0) + jnp.log2(1.0 + jnp.exp2(neg_abs))
        m = valid(u[1])
        if m is not None:
            s = jnp.where(m, s, 0.0)
        return z - s, jnp.dot(s.astype(BF16), suffix_ones, preferred_element_type=F32)

    def weights(u, base, cs, running):
        r0 = first_row(u[1])
        wc = jnp.exp2(base - (cs[:, :LANES] + running[r0:, :]))
        m = valid(u[1])
        if m is not None:
            wc = jnp.where(m, wc, 0.0)
        wc = wc.astype(BF16)
        later = running[r0:, :] + cs[:, LANES:]
        if r0:
            wc = jnp.concatenate([jnp.zeros((r0, LANES), BF16), wc], axis=0)
            later = jnp.concatenate([running[:r0, :], later], axis=0)
        return wc, later

    z_of, lk_of = {}, {}
    running, parts = None, [None] * nsub
    for t in range(len(units) + 2 * SB_PIPE_SKEW):
        if t < len(units):
            z_of[t] = scores(units[t])
        tb = t - SB_PIPE_SKEW
        if 0 <= tb < len(units):
            lk_of[tb] = log_keep(units[tb], z_of.pop(tb))
        tc = t - 2 * SB_PIPE_SKEW
        if 0 <= tc < len(units):
            h, c = units[tc]
            if c == nsub - 1:
                running = later_ref[h]
            base, cs = lk_of.pop(tc)
            parts[c], running = weights(units[tc], base, cs, running)
            if c == 0:
                later_ref[h] = running
                vb = v_ref[0, pl.ds(start, qb), h * LANES:(h + 1) * LANES]
                acc_ref[h] += jnp.dot(jnp.concatenate(parts, axis=1), vb, preferred_element_type=F32)


def _sb_kernel(q_ref, k_ref, v_ref, o_ref, acc_ref, later_ref, *, front_pad, n_heads):
    i = pl.program_id(2)
    heads = range(n_heads)
    q = [q_ref[0, :, h * LANES:(h + 1) * LANES] for h in heads]
    rr = lax.broadcasted_iota(jnp.int32, (LANES, 2 * LANES), 0)
    cc = lax.broadcasted_iota(jnp.int32, (LANES, 2 * LANES), 1)
    suffix_ones = jnp.where((cc >= LANES) | (rr > cc), 1.0, 0.0).astype(BF16)
    acc_ref[...] = jnp.zeros_like(acc_ref)
    later_ref[...] = jnp.zeros_like(later_ref)
    block = functools.partial(_sb_block, q=q, k_ref=k_ref, v_ref=v_ref, acc_ref=acc_ref,
                              later_ref=later_ref, suffix_ones=suffix_ones, heads=heads)

    @pl.when(i == 0)
    def _():
        block(0, causal=True, pad=front_pad)

    @pl.when(i > 0)
    def _():
        block(i, causal=True, pad=None)

        def interior(jj, carry):
            block(i - jj, causal=False, pad=None)
            return carry

        lax.fori_loop(1, i, interior, 0)
        block(0, causal=False, pad=front_pad)

    for h in heads:
        o_ref[0, :, h * LANES:(h + 1) * LANES] = acc_ref[h].astype(o_ref.dtype)


def _sb_attention(q, k, v, *, front_pad):
    bsz, p_rows, d = q.shape
    qb = _largest_tile(p_rows, SB_BLOCK_ROWS, LANES)
    assert front_pad <= qb
    n_heads = _largest_tile(d // SB_HEAD, SB_HEADS_PER_STEP, 1)
    width = n_heads * SB_HEAD
    kern = functools.partial(_sb_kernel, front_pad=front_pad, n_heads=n_heads)
    kv_spec = pl.BlockSpec((1, p_rows, width), lambda b, h, i: (b, 0, h))
    q_spec = pl.BlockSpec((1, qb, width), lambda b, h, i: (b, i, h))
    return pl.pallas_call(
        kern,
        out_shape=jax.ShapeDtypeStruct((bsz, p_rows, d), BF16),
        grid=(bsz, d // width, p_rows // qb),
        in_specs=[q_spec, kv_spec, kv_spec],
        out_specs=q_spec,
        scratch_shapes=[pltpu.VMEM((n_heads, qb, SB_HEAD), F32), pltpu.VMEM((n_heads, qb, SB_HEAD), F32)],
        compiler_params=_params("parallel", "parallel", "arbitrary"),
        name="stick_breaking_attention",
    )(q, k, v)


def _rwkv_time_mix(h, v_first, norm_g, mu, w_r, w_k, w_v, w_o, dec_w0, dec_w1, dec_w2, a_w0, a_w1, a_w2,
                   g_w1, g_w2, k_k, k_a, r_k, gn_w, gn_b, vres, *, bsz, p_rows, front_pad):
    t, d = h.shape
    xr, xw, xk, xv, xa, xg = _token_mix(h, norm_g, mu, seq_rows=p_rows, front_pad=front_pad)
    v = _matmul(xv, w_v)
    tensors = [_matmul(xr, w_r), _matmul(xk, w_k), v,
               _lora(xw, dec_w1, dec_w2, "tanh"), _lora(xa, a_w1, a_w2, "none"),
               _lora(xg, g_w1, g_w2, "sigmoid")]
    v_w0 = jnp.zeros_like(dec_w0)
    if vres is None:
        v_first = v
    else:
        v_w0, v_w1, v_w2 = vres
        tensors += [_lora(xv, v_w1, v_w2, "none"), v_first]
    vecs = jnp.stack([dec_w0, a_w0, k_k, k_a, r_k.reshape(d), gn_w, gn_b, v_w0])
    gated = _rwkv_core([u.reshape(bsz, p_rows, d) for u in tensors], vecs, vres is not None)
    return _matmul(gated.reshape(t, d), w_o), v_first


def kernel(x, meta_tokens, mix_norm_g, ffn_norm_g, ffn_w_gate, ffn_w_up, ffn_w_down, rwkv_mu, rwkv_w_r, rwkv_w_k, rwkv_w_v, rwkv_w_o, rwkv_dec_w0, rwkv_dec_w1, rwkv_dec_w2, rwkv_a_w0, rwkv_a_w1, rwkv_a_w2, rwkv_g_w1, rwkv_g_w2, rwkv_k_k, rwkv_k_a, rwkv_r_k, rwkv_gn_w, rwkv_gn_b, rwkv_v_w0, rwkv_v_w1, rwkv_v_w2, kv_norm_g, sb_w_k, sb_w_v, sb_k_gain, sb_w_q, sb_q_gain, sb_w_o):
    bsz, seq, d = x.shape
    depth = mix_norm_g.shape[0]
    n_a = rwkv_mu.shape[0]
    front_pad = (-N_META) % BLOCK
    p_rows = front_pad + N_META + seq
    assert p_rows % BLOCK == 0 and p_rows % RWKV_CHUNK == 0
    t = bsz * p_rows

    meta = jnp.broadcast_to(meta_tokens[None].astype(x.dtype), (bsz, N_META, d))
    h = jnp.concatenate([jnp.zeros((bsz, front_pad, d), x.dtype), meta, x], axis=1).reshape(t, d)

    v_first = None
    k_sh = v_sh = None
    for layer in range(depth):
        if layer < n_a:
            i = layer
            vres = None if i == 0 else (rwkv_v_w0[i - 1], rwkv_v_w1[i - 1], rwkv_v_w2[i - 1])
            mix, v_first = _rwkv_time_mix(
                h, v_first, mix_norm_g[layer], rwkv_mu[i], rwkv_w_r[i], rwkv_w_k[i], rwkv_w_v[i],
                rwkv_w_o[i], rwkv_dec_w0[i], rwkv_dec_w1[i], rwkv_dec_w2[i], rwkv_a_w0[i], rwkv_a_w1[i],
                rwkv_a_w2[i], rwkv_g_w1[i], rwkv_g_w2[i], rwkv_k_k[i], rwkv_k_a[i], rwkv_r_k[i],
                rwkv_gn_w[i], rwkv_gn_b[i], vres, bsz=bsz, p_rows=p_rows, front_pad=front_pad)
        else:
            j = layer - n_a
            to3 = lambda u: u.reshape(bsz, p_rows, d)
            if j == 0:
                kvn = _rms_cast(h, kv_norm_g, seq_rows=p_rows)
                k_sh = to3(_matmul(kvn, sb_w_k, BF16, head_gain=sb_k_gain))
                v_sh = to3(_matmul(kvn, sb_w_v, BF16))
            hn = _rms_cast(h, mix_norm_g[layer], seq_rows=p_rows)
            q = _matmul(hn, sb_w_q[j], BF16, head_gain=sb_q_gain[j], head_scale=SB_HEAD ** -0.5 * LOG2E)
            o = _sb_attention(to3(q), k_sh, v_sh, front_pad=front_pad)
            mix = _matmul(o.reshape(t, d), sb_w_o[j])
        h = _ffn(h, mix, ffn_norm_g[layer], ffn_w_gate[layer], ffn_w_up[layer], ffn_w_down[layer],
                 seq_rows=p_rows, front_pad=front_pad)
    return h.reshape(bsz, p_rows, d)[:, front_pad + N_META:]
```

```python
import functools

import jax
import jax.numpy as jnp
from jax import lax
from jax.experimental import pallas as pl
from jax.experimental.pallas import tpu as pltpu

N_META = 16
BLOCK = 128
RWKV_HEAD = 64
SB_HEAD = 128
RMS_EPS = 1e-6
GN_EPS = 64e-5

LANES = 128
SUBLANES = 8
TOKEN_MIX_GROUP = 16
RWKV_CHUNK = 64
RWKV_PAIRS_PER_STEP = 16
SB_BLOCK_ROWS = 384
SB_HEADS_PER_STEP = 4
SB_PIPE_SKEW = 2
LOG2E = 1.4426950408889634
VMEM_LIMIT_BYTES = 56 * 1024 * 1024

F32 = jnp.float32
BF16 = jnp.bfloat16


def _largest_tile(n, cap, quantum):
    best = None
    for t in range(quantum, min(n, cap) + 1, quantum):
        if n % t == 0:
            best = t
    assert best is not None, (n, cap, quantum)
    return best


def _params(*sem):
    return pltpu.CompilerParams(dimension_semantics=sem, vmem_limit_bytes=VMEM_LIMIT_BYTES)


def _rms_rows(x, g):
    return x * lax.rsqrt(jnp.mean(x * x, axis=-1, keepdims=True) + RMS_EPS) * g


def _softplus(x):
    return jnp.maximum(x, 0.0) + jnp.log(1.0 + jnp.exp(-jnp.abs(x)))


def _mm_kernel(x_ref, w_ref, o_ref):
    o_ref[...] = jnp.dot(x_ref[...], w_ref[...], preferred_element_type=F32).astype(o_ref.dtype)


def _mm_residual_kernel(x_ref, w_ref, res_ref, o_ref):
    o_ref[...] = res_ref[...] + jnp.dot(x_ref[...], w_ref[...], preferred_element_type=F32)


def _mm_head_rms_kernel(x_ref, w_ref, gain_ref, o_ref, *, scale):
    acc = jnp.dot(x_ref[...], w_ref[...], preferred_element_type=F32)
    gain = gain_ref[...] * scale
    for c in range(acc.shape[1] // SB_HEAD):
        sl = slice(c * SB_HEAD, (c + 1) * SB_HEAD)
        o_ref[:, sl] = _rms_rows(acc[:, sl], gain).astype(o_ref.dtype)


def _matmul(x, w, out_dtype=F32, head_gain=None, head_scale=1.0, residual=None):
    m, k = x.shape
    n = w.shape[1]
    tm = _largest_tile(m, 1056, 16)
    tn = _largest_tile(n, 1024, LANES)
    in_specs = [pl.BlockSpec((tm, k), lambda j, i: (i, 0)),
                pl.BlockSpec((k, tn), lambda j, i: (0, j))]
    args = [x.astype(BF16), w.astype(BF16)]
    assert head_gain is None or residual is None
    if residual is not None:
        kern = _mm_residual_kernel
        in_specs.append(pl.BlockSpec((tm, tn), lambda j, i: (i, j)))
        args.append(residual)
    elif head_gain is None:
        kern = _mm_kernel
    else:
        kern = functools.partial(_mm_head_rms_kernel, scale=head_scale)
        in_specs.append(pl.BlockSpec((1, SB_HEAD), lambda j, i: (0, 0)))
        args.append(head_gain.reshape(1, SB_HEAD))
    return pl.pallas_call(
        kern,
        out_shape=jax.ShapeDtypeStruct((m, n), out_dtype),
        grid=(n // tn, m // tm),
        in_specs=in_specs,
        out_specs=pl.BlockSpec((tm, tn), lambda j, i: (i, j)),
        compiler_params=_params("parallel", "parallel"),
        name="matmul",
    )(*args)


def _lora_kernel(x_ref, w1_ref, w2_ref, o_ref, *, act):
    mid = jnp.dot(x_ref[...], w1_ref[...], preferred_element_type=F32)
    if act == "tanh":
        mid = jnp.tanh(mid)
    elif act == "sigmoid":
        mid = jax.nn.sigmoid(mid)
    o_ref[...] = jnp.dot(mid.astype(BF16), w2_ref[...], preferred_element_type=F32)


def _lora(x, w1, w2, act):
    m, k = x.shape
    n = w2.shape[1]
    rank = w1.shape[1]
    rp = -(-rank // LANES) * LANES
    w1p = jnp.pad(w1, ((0, 0), (0, rp - rank))).astype(BF16)
    w2p = jnp.pad(w2, ((0, rp - rank), (0, 0))).astype(BF16)
    tm = _largest_tile(m, 1056, 16)
    return pl.pallas_call(
        functools.partial(_lora_kernel, act=act),
        out_shape=jax.ShapeDtypeStruct((m, n), F32),
        grid=(m // tm,),
        in_specs=[pl.BlockSpec((tm, k), lambda i: (i, 0)),
                  pl.BlockSpec((k, rp), lambda i: (0, 0)),
                  pl.BlockSpec((rp, n), lambda i: (0, 0))],
        out_specs=pl.BlockSpec((tm, n), lambda i: (i, 0)),
        compiler_params=_params("parallel"),
        name="lora",
    )(x, w1p, w2p)


def _rms_cast_kernel(h_ref, g_ref, o_ref):
    o_ref[...] = _rms_rows(h_ref[...], g_ref[...]).astype(o_ref.dtype)


def _rms_cast(h, g, *, seq_rows):
    m, d = h.shape
    tm = _largest_tile(seq_rows, 1056, 16)
    spec = pl.BlockSpec((tm, d), lambda i: (i, 0))
    return pl.pallas_call(
        _rms_cast_kernel,
        out_shape=jax.ShapeDtypeStruct((m, d), BF16),
        grid=(m // tm,),
        in_specs=[spec, pl.BlockSpec((1, d), lambda i: (0, 0))],
        out_specs=spec,
        compiler_params=_params("parallel"),
        name="rms_cast",
    )(h, g.reshape(1, d))


def _token_mix_kernel(h_ref, tail_ref, g_ref, mu_ref, *o_refs, tiles_per_seq, front_pad):
    g = g_ref[...]
    tm = h_ref.shape[0]
    grp = TOKEN_MIX_GROUP
    row0 = lax.rem(pl.program_id(0), tiles_per_seq) * tm
    sub = lax.broadcasted_iota(jnp.int32, (grp, 1), 0)

    def body(r, prev_tail):
        start = pl.multiple_of(r * grp, grp)
        hn = _rms_rows(h_ref[pl.ds(start, grp), :], g)
        shifted = jnp.where(sub == 0, prev_tail[SUBLANES - 1:SUBLANES, :], pltpu.roll(hn, 1, 0))
        xx = shifted - hn
        keep = row0 + start + sub >= front_pad
        for i, o_ref in enumerate(o_refs):
            o_ref[pl.ds(start, grp), :] = jnp.where(keep, hn + xx * mu_ref[i:i + 1, :], 0.0).astype(o_ref.dtype)
        return hn[grp - SUBLANES:, :]

    lax.fori_loop(0, tm // grp, body, _rms_rows(tail_ref[...], g))


def _token_mix(h, g, mu, *, seq_rows, front_pad):
    m, d = h.shape
    n_mix = mu.shape[0]
    tm = _largest_tile(seq_rows, 528, TOKEN_MIX_GROUP)
    spec = pl.BlockSpec((tm, d), lambda i: (i, 0))
    tail_spec = pl.BlockSpec((SUBLANES, d), lambda i: (jnp.maximum(i * (tm // SUBLANES) - 1, 0), 0))
    kern = functools.partial(_token_mix_kernel, tiles_per_seq=seq_rows // tm, front_pad=front_pad)
    return pl.pallas_call(
        kern,
        out_shape=[jax.ShapeDtypeStruct((m, d), BF16)] * n_mix,
        grid=(m // tm,),
        in_specs=[spec, tail_spec, pl.BlockSpec((1, d), lambda i: (0, 0)),
                  pl.BlockSpec((n_mix, d), lambda i: (0, 0))],
        out_specs=[spec] * n_mix,
        compiler_params=_params("parallel"),
        name="token_mix",
    )(h, h, g.reshape(1, d), mu)


def _ffn_kernel(h_ref, g_ref, wg_ref, wu_ref, wd_ref, o_ref, xn_ref, *, tiles_per_seq, front_pad):
    f = pl.program_id(1)

    @pl.when(f == 0)
    def _():
        h = h_ref[...]
        xn_ref[...] = _rms_rows(h, g_ref[...]).astype(BF16)
        o_ref[...] = h

    xn = xn_ref[...]
    gate = jnp.dot(xn, wg_ref[...], preferred_element_type=F32)
    up = jnp.dot(xn, wu_ref[...], preferred_element_type=F32)
    act = (gate * jax.nn.sigmoid(gate) * up).astype(BF16)
    o_ref[...] += jnp.dot(act, wd_ref[...], preferred_element_type=F32)

    @pl.when(f == pl.num_programs(1) - 1)
    def _():
        tm = o_ref.shape[0]
        row = lax.rem(pl.program_id(0), tiles_per_seq) * tm + lax.broadcasted_iota(jnp.int32, (tm, 1), 0)
        o_ref[...] = jnp.where(row >= front_pad, o_ref[...], 0.0)


def _ffn(h, g, wg, wu, wd, *, seq_rows, front_pad):
    m, d = h.shape
    dff = wg.shape[1]
    tm = _largest_tile(seq_rows, 528, 16)
    tf = _largest_tile(dff, 512, LANES)
    kern = functools.partial(_ffn_kernel, tiles_per_seq=seq_rows // tm, front_pad=front_pad)
    row_spec = pl.BlockSpec((tm, d), lambda i, f: (i, 0))
    return pl.pallas_call(
        kern,
        out_shape=jax.ShapeDtypeStruct((m, d), F32),
        grid=(m // tm, dff // tf),
        in_specs=[row_spec,
                  pl.BlockSpec((1, d), lambda i, f: (0, 0)),
                  pl.BlockSpec((d, tf), lambda i, f: (0, f)),
                  pl.BlockSpec((d, tf), lambda i, f: (0, f)),
                  pl.BlockSpec((tf, d), lambda i, f: (f, 0))],
        out_specs=row_spec,
        scratch_shapes=[pltpu.VMEM((tm, d), BF16)],
        compiler_params=_params("parallel", "arbitrary"),
        name="swiglu_block",
    )(h, g.reshape(1, d), wg.astype(BF16), wu.astype(BF16), wd.astype(BF16))


def _split2(x):
    hi = x.astype(BF16)
    return hi, (x - hi.astype(F32)).astype(BF16)


def _split3(x):
    hi = x.astype(BF16)
    r1 = x - hi.astype(F32)
    mid = r1.astype(BF16)
    lo = (r1 - mid.astype(F32)).astype(BF16)
    return hi, mid, lo


def _rwkv_group_chunk(inputs, vecs, states, consts, use_vres):
    lane_h0, l_incl, m_strict, m_incl, eye, level_masks, head_ones = consts
    pairs = range(len(inputs))
    c = inputs[0][0].shape[0]
    c2 = 2 * c
    nt = lambda x, y: lax.dot_general(x, y, (((1,), (1,)), ((), ())), preferred_element_type=F32)
    nn = lambda x, y: jnp.dot(x, y, preferred_element_type=F32)

    def head_sum(x):
        hi, lo = _split2(x)
        return nn(hi, head_ones) + nn(lo, head_ones)

    def stack(x):
        return jnp.concatenate([jnp.where(lane_h0, x, 0.0), jnp.where(lane_h0, 0.0, x)], axis=0)

    lws, cums, kk_raw, kk_ss = [], [], [], []
    for p in pairs:
        dec = inputs[p][3]
        dec_w0, k_k = vecs[p][0], vecs[p][2]
        wl = -_softplus(-(dec_w0 + dec)) - 0.5
        lw = -jnp.exp(wl)
        hi, mid, lo = _split3(lw)
        cums.append(nn(l_incl, hi) + nn(l_incl, mid) + nn(l_incl, lo))
        lws.append(lw)
        kk0 = inputs[p][1] * k_k
        kk_raw.append(kk0)
        kk_ss.append(head_sum(kk0 * kk0))

    a_s, r_s, v_s, bk_hat, bk_til, decay_all, v_mix, rk = [], [], [], [], [], [], [], []
    for p in pairs:
        r, k, v = inputs[p][0], inputs[p][1], inputs[p][2]
        _, a_w0, _, k_a, r_k = vecs[p][:5]
        al = jax.nn.sigmoid(a_w0 + inputs[p][4])
        kk = kk_raw[p] * lax.rsqrt(jnp.maximum(kk_ss[p], 1e-24))
        k = k * (1.0 + (al - 1.0) * k_a)
        if use_vres:
            v = v + (inputs[p][7] - v) * jax.nn.sigmoid(vecs[p][7] + inputs[p][6])
        v_mix.append(v)
        rk.append(r * k * r_k)
        cum, lw = cums[p], lws[p]
        total = cum[c - 1:c, :]
        e_pos = jnp.exp(cum)
        e_prev = jnp.exp(cum - lw)
        e_neg = jnp.exp(-cum)
        e_rem = jnp.exp(total - cum)
        b = kk * al
        a_s.append(stack(-kk * e_prev).astype(BF16))
        r_s.append(stack(r * e_pos).astype(BF16))
        v_s.append(stack(v).astype(BF16))
        bk_hat.append(jnp.concatenate([stack(b * e_rem), stack(k * e_rem)], axis=0).astype(BF16))
        bt = (b * e_neg).astype(BF16)
        kt = (k * e_neg).astype(BF16)
        bk_til.append(jnp.concatenate([bt, bt, kt, kt], axis=0))
        decay_all.append(jnp.exp(total))

    s_bf = [states[p].astype(BF16) for p in pairs]
    gram = [nt(jnp.concatenate([a_s[p], r_s[p]], axis=0), bk_til[p]) for p in pairs]
    n = [jnp.where(m_strict, gram[p][:c2, :c2], 0.0) for p in pairs]
    n_bf = [x.astype(BF16) for x in n]
    a_ak = [jnp.where(m_strict, gram[p][:c2, c2:], 0.0).astype(BF16) for p in pairs]
    a_rb = [jnp.where(m_incl, gram[p][c2:, :c2], 0.0).astype(BF16) for p in pairs]
    a_rk = [jnp.where(m_incl, gram[p][c2:, c2:], 0.0).astype(BF16) for p in pairs]

    zero_bf = jnp.zeros((), BF16)
    n0 = [jnp.where(level_masks[0], n[p], 0.0) for p in pairs]
    n0_bf = [x.astype(BF16) for x in n0]
    p2 = [nn(n0_bf[p], n0_bf[p]) for p in pairs]
    rhs = [nt(a_s[p], s_bf[p]) + nn(a_ak[p], v_s[p]) for p in pairs]
    p2_bf = [x.astype(BF16) for x in p2]
    p4 = [nn(p2_bf[p], p2_bf[p]) for p in pairs]
    x = [eye + n0[p] + p2[p] + nn(n0_bf[p], p2_bf[p]) for p in pairs]
    y_part = [nt(r_s[p], s_bf[p]) + nn(a_rk[p], v_s[p]) for p in pairs]
    bonus_dot = [head_sum(rk[p]) for p in pairs]
    x = [x[p] + nn(x[p].astype(BF16), p4[p].astype(BF16)) for p in pairs]
    for off_mask in level_masks[1:]:
        x_bf = [x[p].astype(BF16) for p in pairs]
        t = [nn(x_bf[p], jnp.where(off_mask, n_bf[p], zero_bf)) for p in pairs]
        x = [x[p] + nn(t[p].astype(BF16), x_bf[p]) for p in pairs]

    sa = [nn(x[p].astype(BF16), rhs[p].astype(BF16)).astype(BF16) for p in pairs]
    ys, s_new = [], []
    for p in pairs:
        y_s = y_part[p] + nn(a_rb[p], sa[p])
        ys.append(y_s[:c, :] + y_s[c:, :])
    for p in pairs:
        lhs_t = jnp.concatenate([sa[p], v_s[p]], axis=0).astype(F32).T.astype(BF16)
        s_new.append(states[p] * decay_all[p] + nn(lhs_t, bk_hat[p]))

    inv_n = 1.0 / RWKV_HEAD
    means = [head_sum(ys[p]) * inv_n for p in pairs]
    cent = [ys[p] - means[p] for p in pairs]
    var = [head_sum(cent[p] * cent[p]) * inv_n for p in pairs]
    outs = []
    for p in pairs:
        gn_w, gn_b = vecs[p][5], vecs[p][6]
        yn = cent[p] * lax.rsqrt(var[p] + GN_EPS) * gn_w + gn_b
        outs.append(((yn + bonus_dot[p] * v_mix[p]) * inputs[p][5]).astype(BF16))
    return outs, s_new


def _rwkv_consts(c):
    c2 = 2 * c
    row = lax.broadcasted_iota(jnp.int32, (c2, c2), 0)
    col = lax.broadcasted_iota(jnp.int32, (c2, c2), 1)

    def same(size):
        sh = size.bit_length() - 1
        return lax.shift_right_logical(row, sh) == lax.shift_right_logical(col, sh)

    same_head = same(c)
    m_strict = same_head & (col < row)
    m_incl = same_head & (col <= row)
    eye = jnp.where(row == col, 1.0, 0.0).astype(F32)
    levels = [same(8)]
    size = 16
    while size <= c:
        levels.append(same(size) & jnp.logical_not(same(size // 2)))
        size *= 2
    lane_h0 = lax.broadcasted_iota(jnp.int32, (c, LANES), 1) < RWKV_HEAD
    rr = lax.broadcasted_iota(jnp.int32, (c, c), 0)
    cc = lax.broadcasted_iota(jnp.int32, (c, c), 1)
    l_incl = jnp.where(cc <= rr, 1.0, 0.0).astype(BF16)
    head_ones = jnp.where(same(RWKV_HEAD), 1.0, 0.0).astype(BF16)
    return lane_h0, l_incl, m_strict, m_incl, eye, levels, head_ones


def _rwkv_kernel(*refs, pairs, n_in, use_vres):
    in_refs, vec_ref, o_ref, s_ref = refs[:n_in], refs[n_in], refs[n_in + 1], refs[n_in + 2]

    @pl.when(pl.program_id(2) == 0)
    def _():
        s_ref[...] = jnp.zeros_like(s_ref)

    consts = _rwkv_consts(in_refs[0].shape[1])
    lanes = [slice(p * LANES, (p + 1) * LANES) for p in range(pairs)]
    inputs = [tuple(ref[0, :, sl] for ref in in_refs) for sl in lanes]
    vecs = [tuple(vec_ref[i:i + 1, sl] for i in range(vec_ref.shape[0])) for sl in lanes]
    outs, s_new = _rwkv_group_chunk(inputs, vecs, [s_ref[p] for p in range(pairs)], consts, use_vres)
    for p in range(pairs):
        o_ref[0, :, lanes[p]] = outs[p]
        s_ref[p] = s_new[p]


def _rwkv_core(tensors, vecs, use_vres):
    bsz, p_rows, d = tensors[0].shape
    c = RWKV_CHUNK
    pairs = _largest_tile(d // LANES, RWKV_PAIRS_PER_STEP, 1)
    width = pairs * LANES
    spec = pl.BlockSpec((1, c, width), lambda b, g, t: (b, t, g))
    n_in = len(tensors)
    return pl.pallas_call(
        functools.partial(_rwkv_kernel, pairs=pairs, n_in=n_in, use_vres=use_vres),
        out_shape=jax.ShapeDtypeStruct((bsz, p_rows, d), BF16),
        grid=(bsz, d // width, p_rows // c),
        in_specs=[spec] * n_in + [pl.BlockSpec((vecs.shape[0], width), lambda b, g, t: (0, g))],
        out_specs=spec,
        scratch_shapes=[pltpu.VMEM((pairs, LANES, LANES), F32)],
        compiler_params=_params("parallel", "parallel", "arbitrary"),
        name="rwkv7_scan",
    )(*tensors, vecs)


def _sb_block(j, q, k_ref, v_ref, acc_ref, later_ref, suffix_ones, heads, *, causal, pad):
    qb = q[0].shape[0]
    nsub = qb // LANES
    start = pl.multiple_of(j * qb, qb)

    def first_row(c):
        return c * LANES if causal else 0

    def valid(c):
        rows = qb - first_row(c)
        row = lax.broadcasted_iota(jnp.int32, (rows, LANES), 0) + first_row(c)
        col = lax.broadcasted_iota(jnp.int32, (rows, LANES), 1) + c * LANES
        use_pad = pad is not None and c * LANES < pad
        if causal and use_pad:
            return (col < row) & (col >= pad)
        if causal:
            return col < row
        if use_pad:
            return col >= pad
        return None

    units = [(h, c) for h in heads for c in reversed(range(nsub))]

    def scores(u):
        h, c = u
        kb = k_ref[0, pl.ds(pl.multiple_of(start + c * LANES, LANES), LANES), h * LANES:(h + 1) * LANES]
        return lax.dot_general(q[h][first_row(c):, :], kb, (((1,), (1,)), ((), ())),
                               preferred_element_type=F32)

    def log_keep(u, z):
        neg_abs = pltpu.bitcast(pltpu.bitcast(z, jnp.uint32) | jnp.uint32(0x80000000), F32)
        s = jnp.maximum(z, 0.0) + jnp.log2(1.0 + jnp.exp2(neg_abs))
        m = valid(u[1])
        if m is not None:
            s = jnp.where(m, s, 0.0)
        return z - s, jnp.dot(s.astype(BF16), suffix_ones, preferred_element_type=F32)

    def weights(u, base, cs, running):
        r0 = first_row(u[1])
        wc = jnp.exp2(base - (cs[:, :LANES] + running[r0:, :]))
        m = valid(u[1])
        if m is not None:
            wc = jnp.where(m, wc, 0.0)
        wc = wc.astype(BF16)
        later = running[r0:, :] + cs[:, LANES:]
        if r0:
            wc = jnp.concatenate([jnp.zeros((r0, LANES), BF16), wc], axis=0)
            later = jnp.concatenate([running[:r0, :], later], axis=0)
        return wc, later

    z_of, lk_of = {}, {}
    running, parts = None, [None] * nsub
    for t in range(len(units) + 2 * SB_PIPE_SKEW):
        if t < len(units):
            z_of[t] = scores(units[t])
        tb = t - SB_PIPE_SKEW
        if 0 <= tb < len(units):
            lk_of[tb] = log_keep(units[tb], z_of.pop(tb))
        tc = t - 2 * SB_PIPE_SKEW
        if 0 <= tc < len(units):
            h, c = units[tc]
            if c == nsub - 1:
                running = later_ref[h]
            base, cs = lk_of.pop(tc)
            parts[c], running = weights(units[tc], base, cs, running)
            if c == 0:
                later_ref[h] = running
                vb = v_ref[0, pl.ds(start, qb), h * LANES:(h + 1) * LANES]
                acc_ref[h] += jnp.dot(jnp.concatenate(parts, axis=1), vb, preferred_element_type=F32)


def _sb_kernel(q_ref, k_ref, v_ref, o_ref, acc_ref, later_ref, *, front_pad, n_heads):
    i = pl.program_id(2)
    heads = range(n_heads)
    q = [q_ref[0, :, h * LANES:(h + 1) * LANES] for h in heads]
    rr = lax.broadcasted_iota(jnp.int32, (LANES, 2 * LANES), 0)
    cc = lax.broadcasted_iota(jnp.int32, (LANES, 2 * LANES), 1)
    suffix_ones = jnp.where((cc >= LANES) | (rr > cc), 1.0, 0.0).astype(BF16)
    acc_ref[...] = jnp.zeros_like(acc_ref)
    later_ref[...] = jnp.zeros_like(later_ref)
    block = functools.partial(_sb_block, q=q, k_ref=k_ref, v_ref=v_ref, acc_ref=acc_ref,
                              later_ref=later_ref, suffix_ones=suffix_ones, heads=heads)

    @pl.when(i == 0)
    def _():
        block(0, causal=True, pad=front_pad)

    @pl.when(i > 0)
    def _():
        block(i, causal=True, pad=None)

        def interior(jj, carry):
            block(i - jj, causal=False, pad=None)
            return carry

        lax.fori_loop(1, i, interior, 0)
        block(0, causal=False, pad=front_pad)

    for h in heads:
        o_ref[0, :, h * LANES:(h + 1) * LANES] = acc_ref[h].astype(o_ref.dtype)


def _sb_attention(q, k, v, *, front_pad):
    bsz, p_rows, d = q.shape
    qb = _largest_tile(p_rows, SB_BLOCK_ROWS, LANES)
    assert front_pad <= qb
    n_heads = _largest_tile(d // SB_HEAD, SB_HEADS_PER_STEP, 1)
    width = n_heads * SB_HEAD
    kern = functools.partial(_sb_kernel, front_pad=front_pad, n_heads=n_heads)
    kv_spec = pl.BlockSpec((1, p_rows, width), lambda b, h, i: (b, 0, h))
    q_spec = pl.BlockSpec((1, qb, width), lambda b, h, i: (b, i, h))
    return pl.pallas_call(
        kern,
        out_shape=jax.ShapeDtypeStruct((bsz, p_rows, d), BF16),
        grid=(bsz, d // width, p_rows // qb),
        in_specs=[q_spec, kv_spec, kv_spec],
        out_specs=q_spec,
        scratch_shapes=[pltpu.VMEM((n_heads, qb, SB_HEAD), F32), pltpu.VMEM((n_heads, qb, SB_HEAD), F32)],
        compiler_params=_params("parallel", "parallel", "arbitrary"),
        name="stick_breaking_attention",
    )(q, k, v)


def _rwkv_time_mix(h, v_first, norm_g, mu, w_r, w_k, w_v, w_o, dec_w0, dec_w1, dec_w2, a_w0, a_w1, a_w2,
                   g_w1, g_w2, k_k, k_a, r_k, gn_w, gn_b, vres, *, bsz, p_rows, front_pad):
    t, d = h.shape
    xr, xw, xk, xv, xa, xg = _token_mix(h, norm_g, mu, seq_rows=p_rows, front_pad=front_pad)
    v = _matmul(xv, w_v)
    tensors = [_matmul(xr, w_r), _matmul(xk, w_k), v,
               _lora(xw, dec_w1, dec_w2, "tanh"), _lora(xa, a_w1, a_w2, "none"),
               _lora(xg, g_w1, g_w2, "sigmoid")]
    v_w0 = jnp.zeros_like(dec_w0)
    if vres is None:
        v_first = v
    else:
        v_w0, v_w1, v_w2 = vres
        tensors += [_lora(xv, v_w1, v_w2, "none"), v_first]
    vecs = jnp.stack([dec_w0, a_w0, k_k, k_a, r_k.reshape(d), gn_w, gn_b, v_w0])
    gated = _rwkv_core([u.reshape(bsz, p_rows, d) for u in tensors], vecs, vres is not None)
    return _matmul(gated.reshape(t, d), w_o, residual=h), v_first


def kernel(x, meta_tokens, mix_norm_g, ffn_norm_g, ffn_w_gate, ffn_w_up, ffn_w_down, rwkv_mu, rwkv_w_r, rwkv_w_k, rwkv_w_v, rwkv_w_o, rwkv_dec_w0, rwkv_dec_w1, rwkv_dec_w2, rwkv_a_w0, rwkv_a_w1, rwkv_a_w2, rwkv_g_w1, rwkv_g_w2, rwkv_k_k, rwkv_k_a, rwkv_r_k, rwkv_gn_w, rwkv_gn_b, rwkv_v_w0, rwkv_v_w1, rwkv_v_w2, kv_norm_g, sb_w_k, sb_w_v, sb_k_gain, sb_w_q, sb_q_gain, sb_w_o):
    bsz, seq, d = x.shape
    depth = mix_norm_g.shape[0]
    n_a = rwkv_mu.shape[0]
    front_pad = (-N_META) % BLOCK
    p_rows = front_pad + N_META + seq
    assert p_rows % BLOCK == 0 and p_rows % RWKV_CHUNK == 0
    t = bsz * p_rows

    meta = jnp.broadcast_to(meta_tokens[None].astype(x.dtype), (bsz, N_META, d))
    h = jnp.concatenate([jnp.zeros((bsz, front_pad, d), x.dtype), meta, x], axis=1).reshape(t, d)

    v_first = None
    k_sh = v_sh = None
    for layer in range(depth):
        if layer < n_a:
            i = layer
            vres = None if i == 0 else (rwkv_v_w0[i - 1], rwkv_v_w1[i - 1], rwkv_v_w2[i - 1])
            mix, v_first = _rwkv_time_mix(
                h, v_first, mix_norm_g[layer], rwkv_mu[i], rwkv_w_r[i], rwkv_w_k[i], rwkv_w_v[i],
                rwkv_w_o[i], rwkv_dec_w0[i], rwkv_dec_w1[i], rwkv_dec_w2[i], rwkv_a_w0[i], rwkv_a_w1[i],
                rwkv_a_w2[i], rwkv_g_w1[i], rwkv_g_w2[i], rwkv_k_k[i], rwkv_k_a[i], rwkv_r_k[i],
                rwkv_gn_w[i], rwkv_gn_b[i], vres, bsz=bsz, p_rows=p_rows, front_pad=front_pad)
        else:
            j = layer - n_a
            to3 = lambda u: u.reshape(bsz, p_rows, d)
            if j == 0:
                kvn = _rms_cast(h, kv_norm_g, seq_rows=p_rows)
                k_sh = to3(_matmul(kvn, sb_w_k, BF16, head_gain=sb_k_gain))
                v_sh = to3(_matmul(kvn, sb_w_v, BF16))
            hn = _rms_cast(h, mix_norm_g[layer], seq_rows=p_rows)
            q = _matmul(hn, sb_w_q[j], BF16, head_gain=sb_q_gain[j], head_scale=SB_HEAD ** -0.5 * LOG2E)
            o = _sb_attention(to3(q), k_sh, v_sh, front_pad=front_pad)
            mix = _matmul(o.reshape(t, d), sb_w_o[j], residual=h)
        h = _ffn(mix, ffn_norm_g[layer], ffn_w_gate[layer], ffn_w_up[layer], ffn_w_down[layer],
                 seq_rows=p_rows, front_pad=front_pad)
    return h.reshape(bsz, p_rows, d)[:, front_pad + N_META:]
```

```python
import functools

import jax
import jax.numpy as jnp
from jax import lax
from jax.experimental import pallas as pl
from jax.experimental.pallas import tpu as pltpu

N_META = 16
BLOCK = 128
RWKV_HEAD = 64
SB_HEAD = 128
RMS_EPS = 1e-6
GN_EPS = 64e-5

LANES = 128
SUBLANES = 8
TOKEN_MIX_GROUP = 16
RWKV_CHUNK = 64
RWKV_PAIRS_PER_STEP = 16
SB_BLOCK_ROWS = 384
SB_HEADS_PER_STEP = 8
SB_PIPE_SKEW = 2
LOG2E = 1.4426950408889634
VMEM_LIMIT_BYTES = 56 * 1024 * 1024

F32 = jnp.float32
BF16 = jnp.bfloat16


def _largest_tile(n, cap, quantum):
    best = None
    for t in range(quantum, min(n, cap) + 1, quantum):
        if n % t == 0:
            best = t
    assert best is not None, (n, cap, quantum)
    return best


def _params(*sem):
    return pltpu.CompilerParams(dimension_semantics=sem, vmem_limit_bytes=VMEM_LIMIT_BYTES)


def _rms_rows(x, g):
    return x * lax.rsqrt(jnp.mean(x * x, axis=-1, keepdims=True) + RMS_EPS) * g


def _softplus(x):
    return jnp.maximum(x, 0.0) + jnp.log(1.0 + jnp.exp(-jnp.abs(x)))


def _mm_kernel(x_ref, w_ref, o_ref):
    o_ref[...] = jnp.dot(x_ref[...], w_ref[...], preferred_element_type=F32).astype(o_ref.dtype)


def _mm_residual_kernel(x_ref, w_ref, res_ref, o_ref):
    o_ref[...] = res_ref[...] + jnp.dot(x_ref[...], w_ref[...], preferred_element_type=F32)


def _mm_head_rms_kernel(x_ref, w_ref, gain_ref, o_ref, *, scale):
    acc = jnp.dot(x_ref[...], w_ref[...], preferred_element_type=F32)
    gain = gain_ref[...] * scale
    for c in range(acc.shape[1] // SB_HEAD):
        sl = slice(c * SB_HEAD, (c + 1) * SB_HEAD)
        o_ref[:, sl] = _rms_rows(acc[:, sl], gain).astype(o_ref.dtype)


def _matmul(x, w, out_dtype=F32, head_gain=None, head_scale=1.0, residual=None):
    m, k = x.shape
    n = w.shape[1]
    tm = _largest_tile(m, 1056, 16)
    tn = _largest_tile(n, 1024, LANES)
    in_specs = [pl.BlockSpec((tm, k), lambda j, i: (i, 0)),
                pl.BlockSpec((k, tn), lambda j, i: (0, j))]
    args = [x.astype(BF16), w.astype(BF16)]
    assert head_gain is None or residual is None
    if residual is not None:
        kern = _mm_residual_kernel
        in_specs.append(pl.BlockSpec((tm, tn), lambda j, i: (i, j)))
        args.append(residual)
    elif head_gain is None:
        kern = _mm_kernel
    else:
        kern = functools.partial(_mm_head_rms_kernel, scale=head_scale)
        in_specs.append(pl.BlockSpec((1, SB_HEAD), lambda j, i: (0, 0)))
        args.append(head_gain.reshape(1, SB_HEAD))
    return pl.pallas_call(
        kern,
        out_shape=jax.ShapeDtypeStruct((m, n), out_dtype),
        grid=(n // tn, m // tm),
        in_specs=in_specs,
        out_specs=pl.BlockSpec((tm, tn), lambda j, i: (i, j)),
        compiler_params=_params("parallel", "parallel"),
        name="matmul",
    )(*args)


def _lora_kernel(x_ref, w1_ref, w2_ref, o_ref, *, act):
    mid = jnp.dot(x_ref[...], w1_ref[...], preferred_element_type=F32)
    if act == "tanh":
        mid = jnp.tanh(mid)
    elif act == "sigmoid":
        mid = jax.nn.sigmoid(mid)
    o_ref[...] = jnp.dot(mid.astype(BF16), w2_ref[...], preferred_element_type=F32)


def _lora(x, w1, w2, act):
    m, k = x.shape
    n = w2.shape[1]
    rank = w1.shape[1]
    rp = -(-rank // LANES) * LANES
    w1p = jnp.pad(w1, ((0, 0), (0, rp - rank))).astype(BF16)
    w2p = jnp.pad(w2, ((0, rp - rank), (0, 0))).astype(BF16)
    tm = _largest_tile(m, 1056, 16)
    return pl.pallas_call(
        functools.partial(_lora_kernel, act=act),
        out_shape=jax.ShapeDtypeStruct((m, n), F32),
        grid=(m // tm,),
        in_specs=[pl.BlockSpec((tm, k), lambda i: (i, 0)),
                  pl.BlockSpec((k, rp), lambda i: (0, 0)),
                  pl.BlockSpec((rp, n), lambda i: (0, 0))],
        out_specs=pl.BlockSpec((tm, n), lambda i: (i, 0)),
        compiler_params=_params("parallel"),
        name="lora",
    )(x, w1p, w2p)


def _rms_cast_kernel(h_ref, g_ref, o_ref):
    o_ref[...] = _rms_rows(h_ref[...], g_ref[...]).astype(o_ref.dtype)


def _rms_cast(h, g, *, seq_rows):
    m, d = h.shape
    tm = _largest_tile(seq_rows, 1056, 16)
    spec = pl.BlockSpec((tm, d), lambda i: (i, 0))
    return pl.pallas_call(
        _rms_cast_kernel,
        out_shape=jax.ShapeDtypeStruct((m, d), BF16),
        grid=(m // tm,),
        in_specs=[spec, pl.BlockSpec((1, d), lambda i: (0, 0))],
        out_specs=spec,
        compiler_params=_params("parallel"),
        name="rms_cast",
    )(h, g.reshape(1, d))


def _token_mix_kernel(h_ref, tail_ref, g_ref, mu_ref, *o_refs, tiles_per_seq, front_pad):
    g = g_ref[...]
    tm = h_ref.shape[0]
    grp = TOKEN_MIX_GROUP
    row0 = lax.rem(pl.program_id(0), tiles_per_seq) * tm
    sub = lax.broadcasted_iota(jnp.int32, (grp, 1), 0)

    def body(r, prev_tail):
        start = pl.multiple_of(r * grp, grp)
        hn = _rms_rows(h_ref[pl.ds(start, grp), :], g)
        shifted = jnp.where(sub == 0, prev_tail[SUBLANES - 1:SUBLANES, :], pltpu.roll(hn, 1, 0))
        xx = shifted - hn
        keep = row0 + start + sub >= front_pad
        for i, o_ref in enumerate(o_refs):
            o_ref[pl.ds(start, grp), :] = jnp.where(keep, hn + xx * mu_ref[i:i + 1, :], 0.0).astype(o_ref.dtype)
        return hn[grp - SUBLANES:, :]

    lax.fori_loop(0, tm // grp, body, _rms_rows(tail_ref[...], g))


def _token_mix(h, g, mu, *, seq_rows, front_pad):
    m, d = h.shape
    n_mix = mu.shape[0]
    tm = _largest_tile(seq_rows, 528, TOKEN_MIX_GROUP)
    spec = pl.BlockSpec((tm, d), lambda i: (i, 0))
    tail_spec = pl.BlockSpec((SUBLANES, d), lambda i: (jnp.maximum(i * (tm // SUBLANES) - 1, 0), 0))
    kern = functools.partial(_token_mix_kernel, tiles_per_seq=seq_rows // tm, front_pad=front_pad)
    return pl.pallas_call(
        kern,
        out_shape=[jax.ShapeDtypeStruct((m, d), BF16)] * n_mix,
        grid=(m // tm,),
        in_specs=[spec, tail_spec, pl.BlockSpec((1, d), lambda i: (0, 0)),
                  pl.BlockSpec((n_mix, d), lambda i: (0, 0))],
        out_specs=[spec] * n_mix,
        compiler_params=_params("parallel"),
        name="token_mix",
    )(h, h, g.reshape(1, d), mu)


def _ffn_kernel(h_ref, g_ref, wg_ref, wu_ref, wd_ref, o_ref, xn_ref, *, tiles_per_seq, front_pad):
    f = pl.program_id(1)

    @pl.when(f == 0)
    def _():
        h = h_ref[...]
        xn_ref[...] = _rms_rows(h, g_ref[...]).astype(BF16)
        o_ref[...] = h

    xn = xn_ref[...]
    gate = jnp.dot(xn, wg_ref[...], preferred_element_type=F32)
    up = jnp.dot(xn, wu_ref[...], preferred_element_type=F32)
    act = (gate * jax.nn.sigmoid(gate) * up).astype(BF16)
    o_ref[...] += jnp.dot(act, wd_ref[...], preferred_element_type=F32)

    @pl.when(f == pl.num_programs(1) - 1)
    def _():
        tm = o_ref.shape[0]
        row = lax.rem(pl.program_id(0), tiles_per_seq) * tm + lax.broadcasted_iota(jnp.int32, (tm, 1), 0)
        o_ref[...] = jnp.where(row >= front_pad, o_ref[...], 0.0)


def _ffn(h, g, wg, wu, wd, *, seq_rows, front_pad):
    m, d = h.shape
    dff = wg.shape[1]
    tm = _largest_tile(seq_rows, 528, 16)
    tf = _largest_tile(dff, 512, LANES)
    kern = functools.partial(_ffn_kernel, tiles_per_seq=seq_rows // tm, front_pad=front_pad)
    row_spec = pl.BlockSpec((tm, d), lambda i, f: (i, 0))
    return pl.pallas_call(
        kern,
        out_shape=jax.ShapeDtypeStruct((m, d), F32),
        grid=(m // tm, dff // tf),
        in_specs=[row_spec,
                  pl.BlockSpec((1, d), lambda i, f: (0, 0)),
                  pl.BlockSpec((d, tf), lambda i, f: (0, f)),
                  pl.BlockSpec((d, tf), lambda i, f: (0, f)),
                  pl.BlockSpec((tf, d), lambda i, f: (f, 0))],
        out_specs=row_spec,
        scratch_shapes=[pltpu.VMEM((tm, d), BF16)],
        compiler_params=_params("parallel", "arbitrary"),
        name="swiglu_block",
    )(h, g.reshape(1, d), wg.astype(BF16), wu.astype(BF16), wd.astype(BF16))


def _split2(x):
    hi = x.astype(BF16)
    return hi, (x - hi.astype(F32)).astype(BF16)


def _split3(x):
    hi = x.astype(BF16)
    r1 = x - hi.astype(F32)
    mid = r1.astype(BF16)
    lo = (r1 - mid.astype(F32)).astype(BF16)
    return hi, mid, lo


def _rwkv_group_chunk(inputs, vecs, states, consts, use_vres):
    lane_h0, l_incl, m_strict, m_incl, eye, level_masks, head_ones = consts
    pairs = range(len(inputs))
    c = inputs[0][0].shape[0]
    c2 = 2 * c
    nt = lambda x, y: lax.dot_general(x, y, (((1,), (1,)), ((), ())), preferred_element_type=F32)
    nn = lambda x, y: jnp.dot(x, y, preferred_element_type=F32)

    def head_sum(x):
        hi, lo = _split2(x)
        return nn(hi, head_ones) + nn(lo, head_ones)

    def stack(x):
        return jnp.concatenate([jnp.where(lane_h0, x, 0.0), jnp.where(lane_h0, 0.0, x)], axis=0)

    lws, cums, kk_raw, kk_ss = [], [], [], []
    for p in pairs:
        dec = inputs[p][3]
        dec_w0, k_k = vecs[p][0], vecs[p][2]
        wl = -_softplus(-(dec_w0 + dec)) - 0.5
        lw = -jnp.exp(wl)
        hi, mid, lo = _split3(lw)
        cums.append(nn(l_incl, hi) + nn(l_incl, mid) + nn(l_incl, lo))
        lws.append(lw)
        kk0 = inputs[p][1] * k_k
        kk_raw.append(kk0)
        kk_ss.append(head_sum(kk0 * kk0))

    a_s, r_s, v_s, bk_hat, bk_til, decay_all, v_mix, rk = [], [], [], [], [], [], [], []
    for p in pairs:
        r, k, v = inputs[p][0], inputs[p][1], inputs[p][2]
        _, a_w0, _, k_a, r_k = vecs[p][:5]
        al = jax.nn.sigmoid(a_w0 + inputs[p][4])
        kk = kk_raw[p] * lax.rsqrt(jnp.maximum(kk_ss[p], 1e-24))
        k = k * (1.0 + (al - 1.0) * k_a)
        if use_vres:
            v = v + (inputs[p][7] - v) * jax.nn.sigmoid(vecs[p][7] + inputs[p][6])
        v_mix.append(v)
        rk.append(r * k * r_k)
        cum, lw = cums[p], lws[p]
        total = cum[c - 1:c, :]
        e_pos = jnp.exp(cum)
        e_prev = jnp.exp(cum - lw)
        e_neg = jnp.exp(-cum)
        e_rem = jnp.exp(total - cum)
        b = kk * al
        a_s.append(stack(-kk * e_prev).astype(BF16))
        r_s.append(stack(r * e_pos).astype(BF16))
        v_s.append(stack(v).astype(BF16))
        bk_hat.append(jnp.concatenate([stack(b * e_rem), stack(k * e_rem)], axis=0).astype(BF16))
        bt = (b * e_neg).astype(BF16)
        kt = (k * e_neg).astype(BF16)
        bk_til.append(jnp.concatenate([bt, bt, kt, kt], axis=0))
        decay_all.append(jnp.exp(total))

    s_bf = [states[p].astype(BF16) for p in pairs]
    gram = [nt(jnp.concatenate([a_s[p], r_s[p]], axis=0), bk_til[p]) for p in pairs]
    n = [jnp.where(m_strict, gram[p][:c2, :c2], 0.0) for p in pairs]
    n_bf = [x.astype(BF16) for x in n]
    a_ak = [jnp.where(m_strict, gram[p][:c2, c2:], 0.0).astype(BF16) for p in pairs]
    a_rb = [jnp.where(m_incl, gram[p][c2:, :c2], 0.0).astype(BF16) for p in pairs]
    a_rk = [jnp.where(m_incl, gram[p][c2:, c2:], 0.0).astype(BF16) for p in pairs]

    zero_bf = jnp.zeros((), BF16)
    n0 = [jnp.where(level_masks[0], n[p], 0.0) for p in pairs]
    n0_bf = [x.astype(BF16) for x in n0]
    p2 = [nn(n0_bf[p], n0_bf[p]) for p in pairs]
    rhs = [nt(a_s[p], s_bf[p]) + nn(a_ak[p], v_s[p]) for p in pairs]
    p2_bf = [x.astype(BF16) for x in p2]
    p4 = [nn(p2_bf[p], p2_bf[p]) for p in pairs]
    x = [eye + n0[p] + p2[p] + nn(n0_bf[p], p2_bf[p]) for p in pairs]
    y_part = [nt(r_s[p], s_bf[p]) + nn(a_rk[p], v_s[p]) for p in pairs]
    bonus_dot = [head_sum(rk[p]) for p in pairs]
    x = [x[p] + nn(x[p].astype(BF16), p4[p].astype(BF16)) for p in pairs]
    for off_mask in level_masks[1:]:
        x_bf = [x[p].astype(BF16) for p in pairs]
        t = [nn(x_bf[p], jnp.where(off_mask, n_bf[p], zero_bf)) for p in pairs]
        x = [x[p] + nn(t[p].astype(BF16), x_bf[p]) for p in pairs]

    sa = [nn(x[p].astype(BF16), rhs[p].astype(BF16)).astype(BF16) for p in pairs]
    ys, s_new = [], []
    for p in pairs:
        y_s = y_part[p] + nn(a_rb[p], sa[p])
        ys.append(y_s[:c, :] + y_s[c:, :])
    for p in pairs:
        lhs_t = jnp.concatenate([sa[p], v_s[p]], axis=0).astype(F32).T.astype(BF16)
        s_new.append(states[p] * decay_all[p] + nn(lhs_t, bk_hat[p]))

    inv_n = 1.0 / RWKV_HEAD
    means = [head_sum(ys[p]) * inv_n for p in pairs]
    cent = [ys[p] - means[p] for p in pairs]
    var = [head_sum(cent[p] * cent[p]) * inv_n for p in pairs]
    outs = []
    for p in pairs:
        gn_w, gn_b = vecs[p][5], vecs[p][6]
        yn = cent[p] * lax.rsqrt(var[p] + GN_EPS) * gn_w + gn_b
        outs.append(((yn + bonus_dot[p] * v_mix[p]) * inputs[p][5]).astype(BF16))
    return outs, s_new


def _rwkv_consts(c):
    c2 = 2 * c
    row = lax.broadcasted_iota(jnp.int32, (c2, c2), 0)
    col = lax.broadcasted_iota(jnp.int32, (c2, c2), 1)

    def same(size):
        sh = size.bit_length() - 1
        return lax.shift_right_logical(row, sh) == lax.shift_right_logical(col, sh)

    same_head = same(c)
    m_strict = same_head & (col < row)
    m_incl = same_head & (col <= row)
    eye = jnp.where(row == col, 1.0, 0.0).astype(F32)
    levels = [same(8)]
    size = 16
    while size <= c:
        levels.append(same(size) & jnp.logical_not(same(size // 2)))
        size *= 2
    lane_h0 = lax.broadcasted_iota(jnp.int32, (c, LANES), 1) < RWKV_HEAD
    rr = lax.broadcasted_iota(jnp.int32, (c, c), 0)
    cc = lax.broadcasted_iota(jnp.int32, (c, c), 1)
    l_incl = jnp.where(cc <= rr, 1.0, 0.0).astype(BF16)
    head_ones = jnp.where(same(RWKV_HEAD), 1.0, 0.0).astype(BF16)
    return lane_h0, l_incl, m_strict, m_incl, eye, levels, head_ones


def _rwkv_kernel(*refs, pairs, n_in, use_vres):
    in_refs, vec_ref, o_ref, s_ref = refs[:n_in], refs[n_in], refs[n_in + 1], refs[n_in + 2]

    @pl.when(pl.program_id(2) == 0)
    def _():
        s_ref[...] = jnp.zeros_like(s_ref)

    consts = _rwkv_consts(in_refs[0].shape[1])
    lanes = [slice(p * LANES, (p + 1) * LANES) for p in range(pairs)]
    inputs = [tuple(ref[0, :, sl] for ref in in_refs) for sl in lanes]
    vecs = [tuple(vec_ref[i:i + 1, sl] for i in range(vec_ref.shape[0])) for sl in lanes]
    outs, s_new = _rwkv_group_chunk(inputs, vecs, [s_ref[p] for p in range(pairs)], consts, use_vres)
    for p in range(pairs):
        o_ref[0, :, lanes[p]] = outs[p]
        s_ref[p] = s_new[p]


def _rwkv_core(tensors, vecs, use_vres):
    bsz, p_rows, d = tensors[0].shape
    c = RWKV_CHUNK
    pairs = _largest_tile(d // LANES, RWKV_PAIRS_PER_STEP, 1)
    width = pairs * LANES
    spec = pl.BlockSpec((1, c, width), lambda b, g, t: (b, t, g))
    n_in = len(tensors)
    return pl.pallas_call(
        functools.partial(_rwkv_kernel, pairs=pairs, n_in=n_in, use_vres=use_vres),
        out_shape=jax.ShapeDtypeStruct((bsz, p_rows, d), BF16),
        grid=(bsz, d // width, p_rows // c),
        in_specs=[spec] * n_in + [pl.BlockSpec((vecs.shape[0], width), lambda b, g, t: (0, g))],
        out_specs=spec,
        scratch_shapes=[pltpu.VMEM((pairs, LANES, LANES), F32)],
        compiler_params=_params("parallel", "parallel", "arbitrary"),
        name="rwkv7_scan",
    )(*tensors, vecs)


def _sb_block(j, q, k_ref, v_ref, acc_ref, later_ref, suffix_ones, heads, *, causal, pad):
    qb = q[0].shape[0]
    nsub = qb // LANES
    start = pl.multiple_of(j * qb, qb)

    def first_row(c):
        return c * LANES if causal else 0

    def valid(c):
        rows = qb - first_row(c)
        row = lax.broadcasted_iota(jnp.int32, (rows, LANES), 0) + first_row(c)
        col = lax.broadcasted_iota(jnp.int32, (rows, LANES), 1) + c * LANES
        use_pad = pad is not None and c * LANES < pad
        if causal and use_pad:
            return (col < row) & (col >= pad)
        if causal:
            return col < row
        if use_pad:
            return col >= pad
        return None

    units = [(h, c) for h in heads for c in reversed(range(nsub))]

    def scores(u):
        h, c = u
        kb = k_ref[0, pl.ds(pl.multiple_of(start + c * LANES, LANES), LANES), h * LANES:(h + 1) * LANES]
        return lax.dot_general(q[h][first_row(c):, :], kb, (((1,), (1,)), ((), ())),
                               preferred_element_type=F32)

    def log_keep(u, z):
        neg_abs = pltpu.bitcast(pltpu.bitcast(z, jnp.uint32) | jnp.uint32(0x80000000), F32)
        s = jnp.maximum(z, 0.0) + jnp.log2(1.0 + jnp.exp2(neg_abs))
        m = valid(u[1])
        if m is not None:
            s = jnp.where(m, s, 0.0)
        return z - s, jnp.dot(s.astype(BF16), suffix_ones, preferred_element_type=F32)

    def weights(u, base, cs, running):
        r0 = first_row(u[1])
        wc = jnp.exp2(base - (cs[:, :LANES] + running[r0:, :]))
        m = valid(u[1])
        if m is not None:
            wc = jnp.where(m, wc, 0.0)
        wc = wc.astype(BF16)
        later = running[r0:, :] + cs[:, LANES:]
        if r0:
            wc = jnp.concatenate([jnp.zeros((r0, LANES), BF16), wc], axis=0)
            later = jnp.concatenate([running[:r0, :], later], axis=0)
        return wc, later

    z_of, lk_of = {}, {}
    running, parts = None, [None] * nsub
    for t in range(len(units) + 2 * SB_PIPE_SKEW):
        if t < len(units):
            z_of[t] = scores(units[t])
        tb = t - SB_PIPE_SKEW
        if 0 <= tb < len(units):
            lk_of[tb] = log_keep(units[tb], z_of.pop(tb))
        tc = t - 2 * SB_PIPE_SKEW
        if 0 <= tc < len(units):
            h, c = units[tc]
            if c == nsub - 1:
                running = later_ref[h]
            base, cs = lk_of.pop(tc)
            parts[c], running = weights(units[tc], base, cs, running)
            if c == 0:
                later_ref[h] = running
                vb = v_ref[0, pl.ds(start, qb), h * LANES:(h + 1) * LANES]
                acc_ref[h] += jnp.dot(jnp.concatenate(parts, axis=1), vb, preferred_element_type=F32)


def _sb_kernel(q_ref, k_ref, v_ref, o_ref, acc_ref, later_ref, *, front_pad, n_heads):
    i = pl.program_id(2)
    heads = range(n_heads)
    q = [q_ref[0, :, h * LANES:(h + 1) * LANES] for h in heads]
    rr = lax.broadcasted_iota(jnp.int32, (LANES, 2 * LANES), 0)
    cc = lax.broadcasted_iota(jnp.int32, (LANES, 2 * LANES), 1)
    suffix_ones = jnp.where((cc >= LANES) | (rr > cc), 1.0, 0.0).astype(BF16)
    acc_ref[...] = jnp.zeros_like(acc_ref)
    later_ref[...] = jnp.zeros_like(later_ref)
    block = functools.partial(_sb_block, q=q, k_ref=k_ref, v_ref=v_ref, acc_ref=acc_ref,
                              later_ref=later_ref, suffix_ones=suffix_ones, heads=heads)

    @pl.when(i == 0)
    def _():
        block(0, causal=True, pad=front_pad)

    @pl.when(i > 0)
    def _():
        block(i, causal=True, pad=None)

        def interior(jj, carry):
            block(i - jj, causal=False, pad=None)
            return carry

        lax.fori_loop(1, i, interior, 0)
        block(0, causal=False, pad=front_pad)

    for h in heads:
        o_ref[0, :, h * LANES:(h + 1) * LANES] = acc_ref[h].astype(o_ref.dtype)


def _sb_attention(q, k, v, *, front_pad):
    bsz, p_rows, d = q.shape
    qb = _largest_tile(p_rows, SB_BLOCK_ROWS, LANES)
    assert front_pad <= qb
    n_heads = _largest_tile(d // SB_HEAD, SB_HEADS_PER_STEP, 1)
    width = n_heads * SB_HEAD
    kern = functools.partial(_sb_kernel, front_pad=front_pad, n_heads=n_heads)
    kv_spec = pl.BlockSpec((1, p_rows, width), lambda b, h, i: (b, 0, h))
    q_spec = pl.BlockSpec((1, qb, width), lambda b, h, i: (b, i, h))
    return pl.pallas_call(
        kern,
        out_shape=jax.ShapeDtypeStruct((bsz, p_rows, d), BF16),
        grid=(bsz, d // width, p_rows // qb),
        in_specs=[q_spec, kv_spec, kv_spec],
        out_specs=q_spec,
        scratch_shapes=[pltpu.VMEM((n_heads, qb, SB_HEAD), F32), pltpu.VMEM((n_heads, qb, SB_HEAD), F32)],
        compiler_params=_params("parallel", "parallel", "arbitrary"),
        name="stick_breaking_attention",
    )(q, k, v)


def _rwkv_time_mix(h, v_first, norm_g, mu, w_r, w_k, w_v, w_o, dec_w0, dec_w1, dec_w2, a_w0, a_w1, a_w2,
                   g_w1, g_w2, k_k, k_a, r_k, gn_w, gn_b, vres, *, bsz, p_rows, front_pad):
    t, d = h.shape
    xr, xw, xk, xv, xa, xg = _token_mix(h, norm_g, mu, seq_rows=p_rows, front_pad=front_pad)
    v = _matmul(xv, w_v)
    tensors = [_matmul(xr, w_r), _matmul(xk, w_k), v,
               _lora(xw, dec_w1, dec_w2, "tanh"), _lora(xa, a_w1, a_w2, "none"),
               _lora(xg, g_w1, g_w2, "sigmoid")]
    v_w0 = jnp.zeros_like(dec_w0)
    if vres is None:
        v_first = v
    else:
        v_w0, v_w1, v_w2 = vres
        tensors += [_lora(xv, v_w1, v_w2, "none"), v_first]
    vecs = jnp.stack([dec_w0, a_w0, k_k, k_a, r_k.reshape(d), gn_w, gn_b, v_w0])
    gated = _rwkv_core([u.reshape(bsz, p_rows, d) for u in tensors], vecs, vres is not None)
    return _matmul(gated.reshape(t, d), w_o, residual=h), v_first


def kernel(x, meta_tokens, mix_norm_g, ffn_norm_g, ffn_w_gate, ffn_w_up, ffn_w_down, rwkv_mu, rwkv_w_r, rwkv_w_k, rwkv_w_v, rwkv_w_o, rwkv_dec_w0, rwkv_dec_w1, rwkv_dec_w2, rwkv_a_w0, rwkv_a_w1, rwkv_a_w2, rwkv_g_w1, rwkv_g_w2, rwkv_k_k, rwkv_k_a, rwkv_r_k, rwkv_gn_w, rwkv_gn_b, rwkv_v_w0, rwkv_v_w1, rwkv_v_w2, kv_norm_g, sb_w_k, sb_w_v, sb_k_gain, sb_w_q, sb_q_gain, sb_w_o):
    bsz, seq, d = x.shape
    depth = mix_norm_g.shape[0]
    n_a = rwkv_mu.shape[0]
    front_pad = (-N_META) % BLOCK
    p_rows = front_pad + N_META + seq
    assert p_rows % BLOCK == 0 and p_rows % RWKV_CHUNK == 0
    t = bsz * p_rows

    meta = jnp.broadcast_to(meta_tokens[None].astype(x.dtype), (bsz, N_META, d))
    h = jnp.concatenate([jnp.zeros((bsz, front_pad, d), x.dtype), meta, x], axis=1).reshape(t, d)

    v_first = None
    k_sh = v_sh = None
    for layer in range(depth):
        if layer < n_a:
            i = layer
            vres = None if i == 0 else (rwkv_v_w0[i - 1], rwkv_v_w1[i - 1], rwkv_v_w2[i - 1])
            mix, v_first = _rwkv_time_mix(
                h, v_first, mix_norm_g[layer], rwkv_mu[i], rwkv_w_r[i], rwkv_w_k[i], rwkv_w_v[i],
                rwkv_w_o[i], rwkv_dec_w0[i], rwkv_dec_w1[i], rwkv_dec_w2[i], rwkv_a_w0[i], rwkv_a_w1[i],
                rwkv_a_w2[i], rwkv_g_w1[i], rwkv_g_w2[i], rwkv_k_k[i], rwkv_k_a[i], rwkv_r_k[i],
                rwkv_gn_w[i], rwkv_gn_b[i], vres, bsz=bsz, p_rows=p_rows, front_pad=front_pad)
        else:
            j = layer - n_a
            to3 = lambda u: u.reshape(bsz, p_rows, d)
            if j == 0:
                kvn = _rms_cast(h, kv_norm_g, seq_rows=p_rows)
                k_sh = to3(_matmul(kvn, sb_w_k, BF16, head_gain=sb_k_gain))
                v_sh = to3(_matmul(kvn, sb_w_v, BF16))
            hn = _rms_cast(h, mix_norm_g[layer], seq_rows=p_rows)
            q = _matmul(hn, sb_w_q[j], BF16, head_gain=sb_q_gain[j], head_scale=SB_HEAD ** -0.5 * LOG2E)
            o = _sb_attention(to3(q), k_sh, v_sh, front_pad=front_pad)
            mix = _matmul(o.reshape(t, d), sb_w_o[j], residual=h)
        h = _ffn(mix, ffn_norm_g[layer], ffn_w_gate[layer], ffn_w_up[layer], ffn_w_down[layer],
                 seq_rows=p_rows, front_pad=front_pad)
    return h.reshape(bsz, p_rows, d)[:, front_pad + N_META:]
```
